```python
import jax, jax.numpy as jnp
from jax import lax
import numpy as np


D_MODEL = 2048
BATCH = 4
SEQ = 4096
DEPTH = 1

GRID_W = 64
CTX_LEN = 256
MIX_WIDTH = D_MODEL
FOURIER_WIDTH = MIX_WIDTH // 2
FOURIER_GROUPS = 4
FOURIER_GROUP_DIM = FOURIER_WIDTH // FOURIER_GROUPS
MLSTM_WIDTH = MIX_WIDTH - FOURIER_WIDTH
MLSTM_HEADS = 4
MLSTM_HEAD_DIM = MLSTM_WIDTH // MLSTM_HEADS
N_DIRS = 2
GATE_COLS = N_DIRS * 2 * MLSTM_HEADS
PROJ_WIDTH = FOURIER_WIDTH + 4 * MLSTM_WIDTH + GATE_COLS
CONV_K = 5
CHUNK = 64
D_FF = 4 * D_MODEL
N_MOD = 6
EPS = 1e-6
POS_BASE = 10000.0

kernel_name = 'hymba_fnet_bimlstm_dit_block'


def rms_norm(x, g):
    xf = x.astype(jnp.float32)
    y = xf * lax.rsqrt(jnp.mean(xf * xf, axis=-1, keepdims=True) + EPS)
    return (y * g.astype(jnp.float32)).astype(x.dtype)


def modulate(h, shift, scale):
    return h * (1.0 + scale) + shift


def sincos_pos_2d(rows, dtype):
    quarter = D_MODEL // 4
    omega = 1.0 / (POS_BASE ** (jnp.arange(quarter, dtype=jnp.float32) / quarter))
    ar = jnp.arange(rows, dtype=jnp.float32)[:, None] * omega
    ac = jnp.arange(GRID_W, dtype=jnp.float32)[:, None] * omega
    row_emb = jnp.broadcast_to(jnp.concatenate([jnp.sin(ar), jnp.cos(ar)], -1)[:, None, :], (rows, GRID_W, 2 * quarter))
    col_emb = jnp.broadcast_to(jnp.concatenate([jnp.sin(ac), jnp.cos(ac)], -1)[None, :, :], (rows, GRID_W, 2 * quarter))
    return jnp.concatenate([row_emb, col_emb], -1).reshape(rows * GRID_W, D_MODEL).astype(dtype)


def centred_dwconv(u, w):
    K, C = w.shape
    return lax.conv_general_dilated(u, w[:, None, :].astype(u.dtype), window_strides=(1,),
                                    padding=[(K // 2, K // 2)],
                                    dimension_numbers=('NWC', 'WIO', 'NWC'),
                                    feature_group_count=C)


def split_proj(p):
    cuts = [FOURIER_WIDTH + i * MLSTM_WIDTH for i in range(5)]
    return jnp.split(p, cuts, axis=-1)


def to_heads(a):
    B, T, _ = a.shape
    return a.reshape(B, T, MLSTM_HEADS, MLSTM_HEAD_DIM).transpose(0, 2, 1, 3)


def mlstm_inputs(p_q, p_k, p_v, p_g, conv_w, gate_b):
    qk = jax.nn.silu(centred_dwconv(jnp.concatenate([p_q, p_k], axis=-1), conv_w))
    q, k = jnp.split(qk, 2, axis=-1)
    q = to_heads(q).astype(jnp.float32) * (MLSTM_HEAD_DIM ** -0.5)
    k = to_heads(k).astype(jnp.float32)
    v = to_heads(p_v).astype(jnp.float32)
    B, T, _ = p_g.shape
    g = p_g.reshape(B, T, N_DIRS, 2, MLSTM_HEADS).astype(jnp.float32) + gate_b.astype(jnp.float32)
    g = g.transpose(2, 3, 0, 4, 1)
    log_i = g[:, 0]
    log_f = jax.nn.log_sigmoid(g[:, 1])
    return q, k, v, log_i, log_f


def mlstm_chunkwise(q, k, v, log_i, log_f, state):
    B, H, T, _ = q.shape
    L = CHUNK
    nc = T // L

    def to_chunks(a):
        return jnp.moveaxis(a.reshape(B, H, nc, L, *a.shape[3:]), 2, 0)

    xs = tuple(to_chunks(a) for a in (q, k, v, log_i, log_f))
    lower = jnp.tril(jnp.ones((L, L), dtype=bool))

    def step(carry, inp):
        C, n, m = carry
        qc, kc, vc, li, lf = inp
        b = jnp.cumsum(lf, axis=-1)
        d_mat = jnp.where(lower, b[..., :, None] - b[..., None, :] + li[..., None, :], -jnp.inf)
        inter = b + m[..., None]
        m_t = jnp.maximum(inter, jnp.max(d_mat, axis=-1))
        a = jnp.exp(inter - m_t)
        s = jnp.einsum('bhtd,bhsd->bhts', qc, kc) * jnp.exp(d_mat - m_t[..., None])
        num = a[..., None] * jnp.einsum('bhtd,bhde->bhte', qc, C) + jnp.einsum('bhts,bhse->bhte', s, vc)
        den = a * jnp.einsum('bhtd,bhd->bht', qc, n) + jnp.sum(s, axis=-1)
        h = num / jnp.maximum(jnp.abs(den), jnp.exp(-m_t))[..., None]
        b_end = b[..., -1]
        g = b_end[..., None] - b + li
        m_new = jnp.maximum(b_end + m, jnp.max(g, axis=-1))
        decay = jnp.exp(b_end + m - m_new)
        kw = kc * jnp.exp(g - m_new[..., None])[..., None]
        C_new = decay[..., None, None] * C + jnp.einsum('bhsd,bhse->bhde', kw, vc)
        n_new = decay[..., None] * n + jnp.sum(kw, axis=2)
        return (C_new, n_new, m_new), h

    state, hs = lax.scan(step, state, xs)
    return jnp.moveaxis(hs, 0, 2).reshape(B, H, T, v.shape[-1]), state


def mlstm_final_state(k, v, log_i, log_f):
    b = jnp.cumsum(log_f, axis=-1)
    b_end = b[..., -1]
    g = b_end[..., None] - b + log_i
    m = jnp.maximum(b_end, jnp.max(g, axis=-1))
    kw = k * jnp.exp(g - m[..., None])[..., None]
    return (jnp.einsum('bhsd,bhse->bhde', kw, v), jnp.sum(kw, axis=2), m)


def mlstm_direction(q, k, v, log_i, log_f, state, reverse):
    if reverse:
        q, k, v, log_i, log_f = (jnp.flip(a, axis=2) for a in (q, k, v, log_i, log_f))
    h, _ = mlstm_chunkwise(q, k, v, log_i, log_f, state)
    return jnp.flip(h, axis=2) if reverse else h


def context_state(k, v, log_i, log_f, reverse):
    if reverse:
        k, v, log_i, log_f = (jnp.flip(a, axis=2) for a in (k, v, log_i, log_f))
    return mlstm_final_state(k, v, log_i, log_f)


def fourier_mix(u):
    B, T, _ = u.shape
    ug = u.reshape(B, T, FOURIER_GROUPS, FOURIER_GROUP_DIM).astype(jnp.float32)
    y = jnp.fft.fft2(ug, axes=(1, 3), norm='ortho').real
    return y.reshape(B, T, FOURIER_WIDTH).astype(u.dtype)


def mixer_output(f, h, o, head_g, w_out):
    B, H, T, dh = h.shape
    hn = rms_norm(h.transpose(0, 2, 1, 3), head_g.reshape(H, dh)).reshape(B, T, H * dh)
    ym = hn * jax.nn.sigmoid(o.astype(jnp.float32))
    y = jnp.concatenate([fourier_mix(f), ym.astype(f.dtype)], axis=-1)
    return y @ w_out


def squared_relu_mlp(h, w1, w2):
    return jnp.square(jax.nn.relu(h @ w1)) @ w2


def hybrid_layer(x, ctx, mod_x, mod_c, g_mix, g_mlp, w_in, conv_w, gate_b, head_g, w_out,
                 w_mlp1, w_mlp2, update_ctx):
    sh1, sc1, gt1, sh2, sc2, gt2 = (m[:, None, :] for m in jnp.split(mod_x, N_MOD, axis=-1))
    csh1, csc1, cgt1, csh2, csc2, cgt2 = jnp.split(mod_c, N_MOD, axis=-1)
    hx = modulate(rms_norm(x, g_mix), sh1, sc1)
    hc = modulate(rms_norm(ctx, g_mix), csh1, csc1)
    fx, qx, kx, vx, ox, gx = split_proj(hx @ w_in)
    fc, qc, kc, vc, oc, gc = split_proj(hc @ w_in)
    q_x, k_x, v_x, li_x, lf_x = mlstm_inputs(qx, kx, vx, gx, conv_w, gate_b)
    q_c, k_c, v_c, li_c, lf_c = mlstm_inputs(qc, kc, vc, gc, conv_w, gate_b)
    st_fwd = context_state(k_c, v_c, li_c[0], lf_c[0], False)
    st_bwd = context_state(k_c, v_c, li_c[1], lf_c[1], True)
    h_x = (mlstm_direction(q_x, k_x, v_x, li_x[0], lf_x[0], st_fwd, False)
           + mlstm_direction(q_x, k_x, v_x, li_x[1], lf_x[1], st_bwd, True))
    x = x + gt1 * mixer_output(fx, h_x, ox, head_g, w_out)
    x = x + gt2 * squared_relu_mlp(modulate(rms_norm(x, g_mlp), sh2, sc2), w_mlp1, w_mlp2)
    if update_ctx:
        B, H, _, dh = k_c.shape
        zero = (jnp.zeros((B, H, dh, dh), jnp.float32), jnp.zeros((B, H, dh), jnp.float32),
                jnp.zeros((B, H), jnp.float32))
        h_c = (mlstm_direction(q_c, k_c, v_c, li_c[0], lf_c[0], zero, False)
               + mlstm_direction(q_c, k_c, v_c, li_c[1], lf_c[1], zero, True))
        ctx = ctx + cgt1 * mixer_output(fc, h_c, oc, head_g, w_out)
        ctx = ctx + cgt2 * squared_relu_mlp(modulate(rms_norm(ctx, g_mlp), csh2, csc2), w_mlp1, w_mlp2)
    return x, ctx


def setup_inputs(seed: int = 0) -> dict:
    key = jax.random.key(seed)
    ks = jax.random.split(key, 16)

    def nrm(k, shape, scale):
        return jax.random.normal(k, shape, jnp.float32) * scale

    gate_base = jnp.stack([jnp.zeros((MLSTM_HEADS,), jnp.float32),
                           jnp.linspace(3.0, 6.0, MLSTM_HEADS, dtype=jnp.float32)])
    return {
        'x': nrm(ks[0], (BATCH, SEQ, D_MODEL), 1.0),
        'c': nrm(ks[1], (BATCH, D_MODEL), 1.0),
        'ctx': nrm(ks[2], (BATCH, CTX_LEN, D_MODEL), 1.0),
        'c_ctx': nrm(ks[3], (D_MODEL,), 1.0),
        'w_mod': nrm(ks[4], (DEPTH, D_MODEL, N_MOD * D_MODEL), 0.5 * D_MODEL ** -0.5),
        'b_mod': nrm(ks[5], (DEPTH, N_MOD * D_MODEL), 0.02),
        'g_mix': 1.0 + nrm(ks[6], (DEPTH, D_MODEL), 0.02),
        'g_mlp': 1.0 + nrm(ks[7], (DEPTH, D_MODEL), 0.02),
        'w_in': nrm(ks[8], (DEPTH, D_MODEL, PROJ_WIDTH), D_MODEL ** -0.5),
        'conv_w': nrm(ks[9], (DEPTH, CONV_K, 2 * MLSTM_WIDTH), CONV_K ** -0.5),
        'gate_b': gate_base[None, None] + nrm(ks[10], (DEPTH, N_DIRS, 2, MLSTM_HEADS), 0.1),
        'head_g': 1.0 + nrm(ks[11], (DEPTH, MLSTM_WIDTH), 0.02),
        'w_out': nrm(ks[12], (DEPTH, MIX_WIDTH, D_MODEL), MIX_WIDTH ** -0.5),
        'w_mlp1': nrm(ks[13], (DEPTH, D_MODEL, D_FF), D_MODEL ** -0.5),
        'w_mlp2': nrm(ks[14], (DEPTH, D_FF, D_MODEL), D_FF ** -0.5),
        'g_final': 1.0 + nrm(ks[15], (D_MODEL,), 0.02),
    }


def reference(x, c, ctx, c_ctx, w_mod, b_mod, g_mix, g_mlp, w_in, conv_w, gate_b, head_g, w_out,
              w_mlp1, w_mlp2, g_final):
    n_tokens = x.shape[1]
    rows = n_tokens // GRID_W
    x = x + sincos_pos_2d(rows, x.dtype)[None]
    for l in range(DEPTH):
        mod_x = jax.nn.silu(c) @ w_mod[l] + b_mod[l]
        mod_c = jax.nn.silu(c_ctx) @ w_mod[l] + b_mod[l]
        x, ctx = hybrid_layer(x, ctx, mod_x, mod_c, g_mix[l], g_mlp[l], w_in[l], conv_w[l], gate_b[l],
                              head_g[l], w_out[l], w_mlp1[l], w_mlp2[l], update_ctx=(l < DEPTH - 1))
    return rms_norm(x, g_final)
```

```python
import functools

import numpy as np
import jax
import jax.numpy as jnp
from jax import lax
from jax.experimental import pallas as pl
from jax.experimental.pallas import tpu as pltpu

F32 = jnp.float32
BF16 = jnp.bfloat16

GRID_W = 64
N_HEADS = 4
N_DIRS = 2
N_GROUPS = 4
N_MOD = 6
EPS = 1e-6
POS_BASE = 10000.0

LANE = 128
SUBLANE = 8
BF16_ROWS = 16
MIB = 1024 * 1024
V7X_VMEM_BYTES = 64 * MIB

MLSTM_CHUNK = 256
DFT_ROWS = 64


def _params(semantics, vmem_mib):
    assert vmem_mib * MIB < V7X_VMEM_BYTES
    return pltpu.CompilerParams(dimension_semantics=semantics, vmem_limit_bytes=vmem_mib * MIB)


def _dot(a, b):
    return jnp.dot(a, b, preferred_element_type=F32)


def _rms(x, gain):
    return x * lax.rsqrt(jnp.mean(x * x, axis=-1, keepdims=True) + EPS) * gain


def _mod_kernel(s_ref, w_ref, b_ref, o_ref):
    s = s_ref[...]
    a = (s * jax.nn.sigmoid(s)).astype(BF16)
    o_ref[...] = _dot(a, w_ref[...].astype(BF16)) + b_ref[...]


def _modulation(c, c_ctx, w_mod, b_mod):
    B, D = c.shape
    N = w_mod.shape[1]
    rows = -(-(B + 1) // SUBLANE) * SUBLANE
    s = jnp.zeros((rows, D), F32).at[:B].set(c).at[B].set(c_ctx)
    tn = 1024
    out = pl.pallas_call(
        _mod_kernel,
        grid=(N // tn,),
        in_specs=[
            pl.BlockSpec((rows, D), lambda j: (0, 0)),
            pl.BlockSpec((D, tn), lambda j: (0, j)),
            pl.BlockSpec((1, tn), lambda j: (0, j)),
        ],
        out_specs=pl.BlockSpec((rows, tn), lambda j: (0, j)),
        out_shape=jax.ShapeDtypeStruct((rows, N), F32),
        compiler_params=_params(("arbitrary",), 40),
        name="modulation",
    )(s, w_mod, b_mod.reshape(1, N))
    return out.reshape(rows, N_MOD, D)


def _pos_tables(rows, d_model):
    quarter = d_model // 4
    omega = 1.0 / (POS_BASE ** (np.arange(quarter, dtype=np.float64) / quarter))

    def tab(n):
        a = np.arange(n, dtype=np.float64)[:, None] * omega
        return np.concatenate([np.sin(a), np.cos(a)], -1).astype(np.float32)

    return jnp.asarray(tab(rows)).reshape(rows, 1, 2 * quarter), jnp.asarray(tab(GRID_W))


def _add_pos(x, row_ref, col_ref, tile_idx):
    tm, d = x.shape
    g = tm // GRID_W
    rows = row_ref[pl.ds(tile_idx * g, g)]
    pos = jnp.concatenate(
        [jnp.broadcast_to(rows, (g, GRID_W, d // 2)),
         jnp.broadcast_to(col_ref[...][None], (g, GRID_W, d // 2))], axis=-1)
    return x + pos.reshape(tm, d)


def _inproj_kernel(x_ref, row_ref, col_ref, mod_ref, g_ref, w_ref, wg_ref, p_ref, gate_ref, *, use_pos, seg):
    x = x_ref[0]
    if use_pos:
        x = _add_pos(x, row_ref, col_ref, pl.program_id(1))
    y = _rms(x, g_ref[...])
    h = (y * (1.0 + mod_ref[0, 1:2, :]) + mod_ref[0, 0:1, :]).astype(BF16)
    for s in range(w_ref.shape[1] // seg):
        p_ref[0, :, s * seg:(s + 1) * seg] = _dot(h, w_ref[:, s * seg:(s + 1) * seg]).astype(BF16)
    gate_ref[0] = _dot(h, wg_ref[...])


def _inproj(x, row_tab, col_tab, mod, mod_row, g, w, wg, *, use_pos, tm):
    B, T, D = x.shape
    N = w.shape[1]
    mod_map = (lambda b, i: (b, 0, 0)) if mod_row is None else (lambda b, i: (mod_row, 0, 0))
    return pl.pallas_call(
        functools.partial(_inproj_kernel, use_pos=use_pos, seg=1024),
        grid=(B, T // tm),
        in_specs=[
            pl.BlockSpec((1, tm, D), lambda b, i: (b, i, 0)),
            pl.BlockSpec(row_tab.shape, lambda b, i: (0, 0, 0)),
            pl.BlockSpec(col_tab.shape, lambda b, i: (0, 0)),
            pl.BlockSpec((1, N_MOD, D), mod_map),
            pl.BlockSpec((1, D), lambda b, i: (0, 0)),
            pl.BlockSpec((D, N), lambda b, i: (0, 0), pipeline_mode=pl.Buffered(1)),
            pl.BlockSpec((D, LANE), lambda b, i: (0, 0)),
        ],
        out_specs=[
            pl.BlockSpec((1, tm, N), lambda b, i: (b, i, 0)),
            pl.BlockSpec((1, tm, LANE), lambda b, i: (b, i, 0)),
        ],
        out_shape=[
            jax.ShapeDtypeStruct((B, T, N), BF16),
            jax.ShapeDtypeStruct((B, T, LANE), F32),
        ],
        compiler_params=_params(("parallel", "parallel"), 58),
        name="inproj_x" if use_pos else "inproj_ctx",
    )(x, row_tab, col_tab, mod, g, w, wg)


def _conv_kernel(x_ref, prev_ref, next_ref, w_ref, s_ref, o_ref):
    i = pl.program_id(1)
    x = x_ref[0].astype(F32)
    tc = x.shape[0]
    halo = prev_ref.shape[1]
    taps = w_ref.shape[0]
    prev = jnp.where(i > 0, prev_ref[0].astype(F32), 0.0)
    nxt = jnp.where(i < pl.num_programs(1) - 1, next_ref[0].astype(F32), 0.0)
    ext = jnp.concatenate([prev, x, nxt], axis=0)
    w = w_ref[...]
    acc = None
    for j in range(taps):
        off = halo - taps // 2 + j
        term = ext[off:off + tc] * w[j:j + 1]
        acc = term if acc is None else acc + term
    o_ref[0] = (acc * jax.nn.sigmoid(acc) * s_ref[...]).astype(BF16)


def _conv_silu(p, conv_w, post_scale, *, col0, wcol0, ncol, tc):
    B, T, _ = p.shape
    cb = 1024
    halo = BF16_ROWS
    taps = conv_w.shape[0]
    nhalo = T // halo
    r = tc // halo
    return pl.pallas_call(
        _conv_kernel,
        grid=(B, T // tc, ncol),
        in_specs=[
            pl.BlockSpec((1, tc, cb), lambda b, i, j: (b, i, col0 + j)),
            pl.BlockSpec((1, halo, cb), lambda b, i, j: (b, jnp.maximum(i * r - 1, 0), col0 + j)),
            pl.BlockSpec((1, halo, cb), lambda b, i, j: (b, jnp.minimum((i + 1) * r, nhalo - 1), col0 + j)),
            pl.BlockSpec((taps, cb), lambda b, i, j: (0, wcol0 + j)),
            pl.BlockSpec((1, cb), lambda b, i, j: (0, wcol0 + j)),
        ],
        out_specs=pl.BlockSpec((1, tc, cb), lambda b, i, j: (b, i, j)),
        out_shape=jax.ShapeDtypeStruct((B, T, ncol * cb), BF16),
        compiler_params=_params(("parallel", "parallel", "parallel"), 40),
        name="conv_silu",
    )(p, p, p, conv_w, post_scale)


def _seg_scan(x, seg, op, reverse):
    n = x.shape[-1]
    pos = lax.broadcasted_iota(jnp.int32, x.shape, 1) % seg
    shift = 1
    while shift < seg:
        if reverse:
            other = pltpu.roll(x, n - shift, 1)
            ok = pos < seg - shift
        else:
            other = pltpu.roll(x, shift, 1)
            ok = pos >= shift
        x = jnp.where(ok, op(x, other), x)
        shift *= 2
    return x


def _stack_rows(pieces, n_rows):
    n = pieces[0].shape[-1]
    sub = lax.broadcasted_iota(jnp.int32, (n_rows, n), 0)
    out = jnp.zeros((n_rows, n), F32)
    for r, p in enumerate(pieces):
        out = jnp.where(sub == r, p, out)
    return out


def _gate_scans(g_ref, bias, seg):
    gt = (g_ref[0] + bias).T
    nd = N_DIRS * N_HEADS
    li = gt[0:nd]
    gf = gt[nd:2 * nd]
    lf = -(jnp.maximum(-gf, 0.0) + jnp.log1p(jnp.exp(-jnp.abs(gf))))
    fwd = lax.broadcasted_iota(jnp.int32, (nd, 1), 0) < N_HEADS
    pre = _seg_scan(lf, seg, jnp.add, False)
    suf = _seg_scan(lf, seg, jnp.add, True)
    b = jnp.where(fwd, pre, suf)
    tot = pre + suf - lf
    w = li - b
    pmax = _seg_scan(w, seg, jnp.maximum, False)
    smax = _seg_scan(w, seg, jnp.maximum, True)
    return b, tot, w, jnp.where(fwd, pmax, smax), jnp.maximum(pmax, smax), fwd


def _prep_kernel(gx_ref, gc_ref, bias_ref, rows_ref, cols_ref, ccols_ref, *, chunk):
    H = N_HEADS
    L = chunk
    T = gx_ref.shape[1]
    nc = T // L
    bias = bias_ref[...]

    _, tot_c, w_c, _, wmax_c, _ = _gate_scans(gc_ref, bias, gc_ref.shape[1])
    m_end_c = jnp.maximum(wmax_c, 0.0)
    kws_c = jnp.exp(w_c - m_end_c)
    m0 = (tot_c + m_end_c)[:, :LANE]
    for h in range(H):
        st = _stack_rows([kws_c[h:h + 1], kws_c[H + h:H + h + 1]], LANE)
        ccols_ref[0, h] = st.T

    b, tot, w, cw, wmax, fwd = _gate_scans(gx_ref, bias, L)

    def run(order):
        m_prev = m0
        out = [None] * nc
        for c in order:
            sl = slice(c * L, (c + 1) * L)
            mp = jnp.concatenate([m_prev] * (L // LANE), axis=1)
            m_t = jnp.maximum(mp, cw[:, sl])
            m_end = jnp.maximum(mp, wmax[:, sl])
            out[c] = (m_t, jnp.exp(mp - m_t), jnp.exp(-(b[:, sl] + m_t)),
                      jnp.exp(w[:, sl] - m_end), jnp.exp(mp - m_end))
            m_prev = (tot[:, sl] + m_end)[:, :LANE]
        return [jnp.concatenate([o[q] for o in out], axis=1) for q in range(5)]

    asc = run(range(nc))
    desc = run(range(nc - 1, -1, -1))
    m_t, a_t, e_t, k_t, dec = [jnp.where(fwd, x, y) for x, y in zip(asc, desc)]

    for h in range(H):
        r0, r1 = slice(h, h + 1), slice(H + h, H + h + 1)
        rows = _stack_rows([w[r0], w[r1], dec[r0], dec[r1]], SUBLANE)
        for c in range(nc):
            rows_ref[0, h, c] = rows[:, c * L:(c + 1) * L]
        st = _stack_rows([m_t[r0], a_t[r0], e_t[r0], k_t[r0], m_t[r1], a_t[r1], e_t[r1], k_t[r1]], LANE)
        cols_ref[0, h] = st.T


def _gate_prep(gx, gc, bias, *, chunk):
    B, T, _ = gx.shape
    C = gc.shape[1]
    H = N_HEADS
    nc = T // chunk
    return pl.pallas_call(
        functools.partial(_prep_kernel, chunk=chunk),
        grid=(B,),
        in_specs=[
            pl.BlockSpec((1, T, LANE), lambda b: (b, 0, 0)),
            pl.BlockSpec((1, C, LANE), lambda b: (b, 0, 0)),
            pl.BlockSpec((1, LANE), lambda b: (0, 0)),
        ],
        out_specs=[
            pl.BlockSpec((1, H, nc, SUBLANE, chunk), lambda b: (b, 0, 0, 0, 0)),
            pl.BlockSpec((1, H, T, LANE), lambda b: (b, 0, 0, 0)),
            pl.BlockSpec((1, H, C, LANE), lambda b: (b, 0, 0, 0)),
        ],
        out_shape=[
            jax.ShapeDtypeStruct((B, H, nc, SUBLANE, chunk), F32),
            jax.ShapeDtypeStruct((B, H, T, LANE), F32),
            jax.ShapeDtypeStruct((B, H, C, LANE), F32),
        ],
        compiler_params=_params(("parallel",), 48),
        name="gate_prep",
    )(gx, gc, bias)


_TN = (((0,), (0,)), ((), ()))
_NT = (((1,), (1,)), ((), ()))


def _mlstm_kernel(q_ref, k_ref, v_ref, o_ref, kc_ref, vc_ref, rows_ref, cols_ref, ccols_ref, hg_ref,
                  y_ref, hbuf, sf_ref, sb_ref, *, chunk):
    L = chunk
    T = q_ref.shape[1]
    nc = T // L
    states = (sf_ref, sb_ref)

    def ext(v):
        return jnp.concatenate([v, jnp.ones((v.shape[0], LANE), BF16)], axis=1)

    vc = ext(vc_ref[0])
    kc = kc_ref[0].astype(F32)
    cc = ccols_ref[0, 0]
    for d in range(N_DIRS):
        kw = (kc * cc[:, d:d + 1]).astype(BF16)
        states[d][...] = lax.dot_general(kw, vc, _TN, preferred_element_type=F32)

    row_i = lax.broadcasted_iota(jnp.int32, (L, L), 0)
    col_i = lax.broadcasted_iota(jnp.int32, (L, L), 1)
    causal = (col_i <= row_i, col_i >= row_i)

    def step(c, d):
        t0 = pl.multiple_of(c * L, L)
        q = q_ref[0, pl.ds(t0, L), :]
        k = k_ref[0, pl.ds(t0, L), :]
        v = ext(v_ref[0, pl.ds(t0, L), :])
        dh = q.shape[1]
        cols = cols_ref[0, 0, pl.ds(t0, L), :]
        m_t, a_t, e_t, k_t = (cols[:, 4 * d + j:4 * d + j + 1] for j in range(4))
        rows = rows_ref[0, 0, c]
        w_row = rows[d:d + 1, :]
        dec = rows[N_DIRS + d:N_DIRS + d + 1, :LANE]
        state = states[d][...]

        scores = lax.dot_general(q, k, _NT, preferred_element_type=F32)
        p = (scores * jnp.exp(jnp.where(causal[d], w_row - m_t, -1e30))).astype(BF16)
        aq = (q.astype(F32) * a_t).astype(BF16)
        out = _dot(p, v) + _dot(aq, state.astype(BF16))
        r = 1.0 / jnp.maximum(jnp.abs(out[:, dh:]), e_t)
        h = out[:, :dh] * jnp.concatenate([r] * (dh // LANE), axis=1)

        kw = (k.astype(F32) * k_t).astype(BF16)
        upd = lax.dot_general(kw, v, _TN, preferred_element_type=F32)
        states[d][...] = state * jnp.concatenate([dec] * (state.shape[1] // LANE), axis=1) + upd
        return h

    def finalize(hs, c):
        t0 = pl.multiple_of(c * L, L)
        og = o_ref[0, pl.ds(t0, L), :].astype(F32)
        y_ref[0, pl.ds(t0, L), :] = (_rms(hs, hg_ref[...]) * jax.nn.sigmoid(og)).astype(BF16)

    def first_half(i, carry):
        j = nc - 1 - i
        hbuf[pl.ds(pl.multiple_of(i * L, L), L), :] = step(i, 0)
        hbuf[pl.ds(pl.multiple_of(j * L, L), L), :] = step(j, 1)
        return carry

    def second_half(i, carry):
        j = nc - 1 - i
        finalize(step(i, 0) + hbuf[pl.ds(pl.multiple_of(i * L, L), L), :], i)
        finalize(step(j, 1) + hbuf[pl.ds(pl.multiple_of(j * L, L), L), :], j)
        return carry

    lax.fori_loop(0, nc // 2, first_half, 0)
    lax.fori_loop(nc // 2, nc, second_half, 0)


def _mlstm(qk, p, kvc, rows, cols, ccols, head_g, *, chunk, v_blk, o_blk):
    B, T, _ = qk.shape
    C = kvc.shape[1]
    H = N_HEADS
    dh = head_g.shape[1] // H
    nc = T // chunk
    assert nc % 2 == 0
    tok = lambda off: pl.BlockSpec((1, T, dh), lambda b, h: (b, 0, off + h))
    ctx = lambda off: pl.BlockSpec((1, C, dh), lambda b, h: (b, 0, off + h))
    return pl.pallas_call(
        functools.partial(_mlstm_kernel, chunk=chunk),
        grid=(B, H),
        in_specs=[
            tok(0), tok(H), tok(v_blk), tok(o_blk), ctx(0), ctx(H),
            pl.BlockSpec((1, 1, nc, SUBLANE, chunk), lambda b, h: (b, h, 0, 0, 0)),
            pl.BlockSpec((1, 1, T, LANE), lambda b, h: (b, h, 0, 0)),
            pl.BlockSpec((1, 1, C, LANE), lambda b, h: (b, h, 0, 0)),
            pl.BlockSpec((1, dh), lambda b, h: (0, h)),
        ],
        out_specs=pl.BlockSpec((1, T, dh), lambda b, h: (b, 0, h)),
        out_shape=jax.ShapeDtypeStruct((B, T, H * dh), BF16),
        scratch_shapes=[
            pltpu.VMEM((T, dh), F32),
            pltpu.VMEM((dh, dh + LANE), F32),
            pltpu.VMEM((dh, dh + LANE), F32),
        ],
        compiler_params=_params(("parallel", "parallel"), 48),
        name="mlstm",
    )(qk, qk, p, p, kvc, kvc, rows, cols, ccols, head_g)


def _dft_tables(T, gd):
    two_pi = 2.0 * np.pi
    ph = (np.arange(gd)[:, None] * np.arange(gd)[None, :]) % gd
    cs = np.concatenate([np.cos(two_pi * ph / gd), np.sin(two_pi * ph / gd)], axis=1)
    t = np.arange(T, dtype=np.int64)[None, :]
    ph1 = (np.arange(T // DFT_ROWS, dtype=np.int64)[:, None] * DFT_ROWS * t) % T
    ph2 = (np.arange(DFT_ROWS, dtype=np.int64)[:, None] * t) % T
    f = lambda fn, ph: jnp.asarray(fn(two_pi * ph / T).astype(np.float32))
    return jnp.asarray(cs.astype(np.float32)), f(np.cos, ph1), f(np.sin, ph1), f(np.cos, ph2), f(np.sin, ph2)


def _cdft_kernel(f_ref, cs_ref, o_ref):
    gd = cs_ref.shape[0]
    for g in range(f_ref.shape[2] // gd):
        pq = _dot(f_ref[0, :, g * gd:(g + 1) * gd], cs_ref[...])
        o_ref[0, 0, :, g * gd:(g + 1) * gd] = pq[:, :gd].astype(BF16)
        o_ref[0, 1, :, g * gd:(g + 1) * gd] = pq[:, gd:].astype(BF16)


def _channel_dft(p, cs, *, fw, tm):
    B, T, _ = p.shape
    gd = cs.shape[0]
    return pl.pallas_call(
        _cdft_kernel,
        grid=(B, T // tm),
        in_specs=[
            pl.BlockSpec((1, tm, fw), lambda b, i: (b, i, 0)),
            pl.BlockSpec((gd, 2 * gd), lambda b, i: (0, 0)),
        ],
        out_specs=pl.BlockSpec((1, 2, tm, fw), lambda b, i: (b, 0, i, 0)),
        out_shape=jax.ShapeDtypeStruct((B, 2, T, fw), BF16),
        compiler_params=_params(("parallel", "parallel"), 32),
        name="channel_dft",
    )(p, cs)


def _sdft_kernel(f1c_ref, f1s_ref, f2c_ref, f2s_ref, pq_ref, y_ref, lhs, *, scale):
    tm = y_ref.shape[1]
    R, T = f2c_ref.shape
    i = pl.program_id(0)

    @pl.when((pl.program_id(1) == 0) & (pl.program_id(2) == 0))
    def _():
        c2 = f2c_ref[...]
        s2 = f2s_ref[...]
        for j in range(tm // R):
            k1 = i * (tm // R) + j
            c1 = f1c_ref[pl.ds(k1, 1), :]
            s1 = f1s_ref[pl.ds(k1, 1), :]
            lhs[j * R:(j + 1) * R, 0:T] = (c1 * c2 - s1 * s2).astype(BF16)
            lhs[j * R:(j + 1) * R, T:2 * T] = (-(s1 * c2 + c1 * s2)).astype(BF16)

    y_ref[0] = (_dot(lhs[...], pq_ref[0]) * scale).astype(BF16)


def _sequence_dft(pq, f1c, f1s, f2c, f2s, *, scale, tm, tn):
    B, two_t, fw = pq.shape
    T = two_t // 2
    whole = lambda a: pl.BlockSpec(a.shape, lambda i, b, n: (0, 0), pipeline_mode=pl.Buffered(1))
    return pl.pallas_call(
        functools.partial(_sdft_kernel, scale=scale),
        grid=(T // tm, B, fw // tn),
        in_specs=[
            whole(f1c), whole(f1s), whole(f2c), whole(f2s),
            pl.BlockSpec((1, two_t, tn), lambda i, b, n: (b, 0, n)),
        ],
        out_specs=pl.BlockSpec((1, tm, tn), lambda i, b, n: (b, i, n)),
        out_shape=jax.ShapeDtypeStruct((B, T, fw), BF16),
        scratch_shapes=[pltpu.VMEM((tm, two_t), BF16)],
        compiler_params=_params(("arbitrary", "arbitrary", "arbitrary"), 48),
        name="sequence_dft",
    )(f1c, f1s, f2c, f2s, pq)


def _outproj_kernel(yf_ref, ym_ref, x_ref, row_ref, col_ref, mod_ref, g_ref, w_ref, x2_ref, hm_ref):
    fw = yf_ref.shape[2]
    x = _add_pos(x_ref[0], row_ref, col_ref, pl.program_id(1))
    mix = _dot(yf_ref[0], w_ref[0:fw, :]) + _dot(ym_ref[0], w_ref[fw:, :])
    x2 = x + mod_ref[0, 2:3, :] * mix
    x2_ref[0] = x2
    y = _rms(x2, g_ref[...])
    hm_ref[0] = (y * (1.0 + mod_ref[0, 4:5, :]) + mod_ref[0, 3:4, :]).astype(BF16)


def _outproj(yf, ym, x, row_tab, col_tab, mod, g, w, *, tm):
    B, T, D = x.shape
    fw, mw = yf.shape[2], ym.shape[2]
    return pl.pallas_call(
        _outproj_kernel,
        grid=(B, T // tm),
        in_specs=[
            pl.BlockSpec((1, tm, fw), lambda b, i: (b, i, 0)),
            pl.BlockSpec((1, tm, mw), lambda b, i: (b, i, 0)),
            pl.BlockSpec((1, tm, D), lambda b, i: (b, i, 0)),
            pl.BlockSpec(row_tab.shape, lambda b, i: (0, 0, 0)),
            pl.BlockSpec(col_tab.shape, lambda b, i: (0, 0)),
            pl.BlockSpec((1, N_MOD, D), lambda b, i: (b, 0, 0)),
            pl.BlockSpec((1, D), lambda b, i: (0, 0)),
            pl.BlockSpec((fw + mw, D), lambda b, i: (0, 0), pipeline_mode=pl.Buffered(1)),
        ],
        out_specs=[
            pl.BlockSpec((1, tm, D), lambda b, i: (b, i, 0)),
            pl.BlockSpec((1, tm, D), lambda b, i: (b, i, 0)),
        ],
        out_shape=[
            jax.ShapeDtypeStruct((B, T, D), F32),
            jax.ShapeDtypeStruct((B, T, D), BF16),
        ],
        compiler_params=_params(("parallel", "parallel"), 52),
        name="outproj",
    )(yf, ym, x, row_tab, col_tab, mod, g, w)


def _mlp_kernel(hm_ref, w1_ref, w2_ref, x2_ref, mod_ref, g_ref, o_ref, acc):
    f = pl.program_id(2)

    @pl.when(f == 0)
    def _():
        acc[...] = jnp.zeros_like(acc)

    hid = jnp.square(jnp.maximum(_dot(hm_ref[0], w1_ref[...]), 0.0)).astype(BF16)
    acc[...] += _dot(hid, w2_ref[...])

    @pl.when(f == pl.num_programs(2) - 1)
    def _():
        o_ref[0] = _rms(x2_ref[0] + mod_ref[0, 5:6, :] * acc[...], g_ref[...])


def _mlp(hm, w1, w2, x2, mod, g, *, tm, tf):
    B, T, D = x2.shape
    dff = w1.shape[1]
    return pl.pallas_call(
        _mlp_kernel,
        grid=(B, T // tm, dff // tf),
        in_specs=[
            pl.BlockSpec((1, tm, D), lambda b, i, f: (b, i, 0)),
            pl.BlockSpec((D, tf), lambda b, i, f: (0, f)),
            pl.BlockSpec((tf, D), lambda b, i, f: (f, 0)),
            pl.BlockSpec((1, tm, D), lambda b, i, f: (b, i, 0)),
            pl.BlockSpec((1, N_MOD, D), lambda b, i, f: (b, 0, 0)),
            pl.BlockSpec((1, D), lambda b, i, f: (0, 0)),
        ],
        out_specs=pl.BlockSpec((1, tm, D), lambda b, i, f: (b, i, 0)),
        out_shape=jax.ShapeDtypeStruct((B, T, D), F32),
        scratch_shapes=[pltpu.VMEM((tm, D), F32)],
        compiler_params=_params(("parallel", "parallel", "arbitrary"), 56),
        name="mlp",
    )(hm, w1, w2, x2, mod, g)


def kernel(x, c, ctx, c_ctx, w_mod, b_mod, g_mix, g_mlp, w_in, conv_w, gate_b, head_g, w_out,
           w_mlp1, w_mlp2, g_final):
    assert w_mod.shape[0] == 1, "single-layer block"
    B, T, D = x.shape
    H = N_HEADS
    n_gate = N_DIRS * 2 * H
    mw = conv_w.shape[2] // 2
    fw = w_in.shape[2] - 4 * mw - n_gate
    gd = fw // N_GROUPS
    dh = mw // H
    assert T % GRID_W == 0 and fw % 1024 == 0 and mw % 1024 == 0 and dh % LANE == 0

    mod = _modulation(c, c_ctx, w_mod[0], b_mod[0])
    row_tab, col_tab = _pos_tables(T // GRID_W, D)
    g_mix2, g_mlp2, g_fin2 = (g.reshape(1, D) for g in (g_mix[0], g_mlp[0], g_final))

    perm = np.array([d * 2 * H + gate * H + h for gate in range(2) for d in range(N_DIRS) for h in range(H)])
    w_main = w_in[0, :, :fw + 4 * mw].astype(BF16)
    w_gate = jnp.pad(w_in[0, :, fw + 4 * mw:][:, perm], ((0, 0), (0, LANE - n_gate))).astype(BF16)
    bias = jnp.pad(gate_b[0].reshape(n_gate)[perm], (0, LANE - n_gate)).reshape(1, LANE)

    p, gx = _inproj(x, row_tab, col_tab, mod, None, g_mix2, w_main, w_gate, use_pos=True, tm=512)
    w_kv = w_main[:, fw + mw:fw + 3 * mw]
    pc, gc = _inproj(ctx, row_tab, col_tab, mod, B, g_mix2, w_kv, w_gate, use_pos=False, tm=ctx.shape[1])

    post = jnp.concatenate([jnp.full((1, mw), dh ** -0.5, F32), jnp.ones((1, mw), F32)], axis=1)
    qk = _conv_silu(p, conv_w[0], post, col0=fw // 1024, wcol0=0, ncol=2 * mw // 1024, tc=512)
    kc = _conv_silu(pc, conv_w[0], post, col0=0, wcol0=mw // 1024, ncol=mw // 1024, tc=ctx.shape[1])
    kvc = jnp.concatenate([kc, pc[:, :, mw:]], axis=-1)

    rows, cols, ccols = _gate_prep(gx, gc, bias, chunk=MLSTM_CHUNK)
    ym = _mlstm(qk, p, kvc, rows, cols, ccols, head_g[0].reshape(1, mw), chunk=MLSTM_CHUNK,
                v_blk=(fw + 2 * mw) // dh, o_blk=(fw + 3 * mw) // dh)

    cs, f1c, f1s, f2c, f2s = _dft_tables(T, gd)
    pq = _channel_dft(p, cs.astype(BF16), fw=fw, tm=512)
    yf = _sequence_dft(pq.reshape(B, 2 * T, fw), f1c, f1s, f2c, f2s, scale=float((T * gd) ** -0.5),
                       tm=512, tn=512)

    x2, hm = _outproj(yf, ym, x, row_tab, col_tab, mod, g_mlp2, w_out[0].astype(BF16), tm=512)
    return _mlp(hm, w_mlp1[0].astype(BF16), w_mlp2[0].astype(BF16), x2, mod, g_fin2, tm=512, tf=1024)
```

```python
import functools

import numpy as np
import jax
import jax.numpy as jnp
from jax import lax
from jax.experimental import pallas as pl
from jax.experimental.pallas import tpu as pltpu

F32 = jnp.float32
BF16 = jnp.bfloat16

GRID_W = 64
N_HEADS = 4
N_DIRS = 2
N_GROUPS = 4
N_MOD = 6
EPS = 1e-6
POS_BASE = 10000.0

LANE = 128
SUBLANE = 8
BF16_ROWS = 16
MIB = 1024 * 1024
V7X_VMEM_BYTES = 64 * MIB

MLSTM_CHUNK = 256
DFT_ROWS = 64


def _params(semantics, vmem_mib):
    assert vmem_mib * MIB < V7X_VMEM_BYTES
    return pltpu.CompilerParams(dimension_semantics=semantics, vmem_limit_bytes=vmem_mib * MIB)


def _dot(a, b):
    return jnp.dot(a, b, preferred_element_type=F32)


def _rms(x, gain):
    return x * lax.rsqrt(jnp.mean(x * x, axis=-1, keepdims=True) + EPS) * gain


def _mod_kernel(s_ref, w_ref, b_ref, o_ref):
    s = s_ref[...]
    a = (s * jax.nn.sigmoid(s)).astype(BF16)
    o_ref[...] = _dot(a, w_ref[...].astype(BF16)) + b_ref[...]


def _modulation(c, c_ctx, w_mod, b_mod):
    B, D = c.shape
    N = w_mod.shape[1]
    rows = -(-(B + 1) // SUBLANE) * SUBLANE
    s = jnp.zeros((rows, D), F32).at[:B].set(c).at[B].set(c_ctx)
    tn = 1024
    out = pl.pallas_call(
        _mod_kernel,
        grid=(N // tn,),
        in_specs=[
            pl.BlockSpec((rows, D), lambda j: (0, 0)),
            pl.BlockSpec((D, tn), lambda j: (0, j)),
            pl.BlockSpec((1, tn), lambda j: (0, j)),
        ],
        out_specs=pl.BlockSpec((rows, tn), lambda j: (0, j)),
        out_shape=jax.ShapeDtypeStruct((rows, N), F32),
        compiler_params=_params(("arbitrary",), 40),
        name="modulation",
    )(s, w_mod, b_mod.reshape(1, N))
    return out.reshape(rows, N_MOD, D)


def _pos_tables(rows, d_model):
    quarter = d_model // 4
    omega = 1.0 / (POS_BASE ** (np.arange(quarter, dtype=np.float64) / quarter))

    def tab(n):
        a = np.arange(n, dtype=np.float64)[:, None] * omega
        return np.concatenate([np.sin(a), np.cos(a)], -1).astype(np.float32)

    return jnp.asarray(tab(rows)).reshape(rows, 1, 2 * quarter), jnp.asarray(tab(GRID_W))


def _add_pos(x, row_ref, col_ref, tile_idx):
    tm, d = x.shape
    g = tm // GRID_W
    rows = row_ref[pl.ds(tile_idx * g, g)]
    pos = jnp.concatenate(
        [jnp.broadcast_to(rows, (g, GRID_W, d // 2)),
         jnp.broadcast_to(col_ref[...][None], (g, GRID_W, d // 2))], axis=-1)
    return x + pos.reshape(tm, d)


def _inproj_kernel(x_ref, row_ref, col_ref, mod_ref, g_ref, w_ref, wg_ref, p_ref, gate_ref, *, use_pos, seg):
    x = x_ref[0]
    if use_pos:
        x = _add_pos(x, row_ref, col_ref, pl.program_id(1))
    y = _rms(x, g_ref[...])
    h = (y * (1.0 + mod_ref[0, 1:2, :]) + mod_ref[0, 0:1, :]).astype(BF16)
    for s in range(w_ref.shape[1] // seg):
        p_ref[0, :, s * seg:(s + 1) * seg] = _dot(h, w_ref[:, s * seg:(s + 1) * seg]).astype(BF16)
    gate_ref[0] = _dot(h, wg_ref[...])


def _inproj(x, row_tab, col_tab, mod, mod_row, g, w, wg, *, use_pos, tm, n_cols, col_blk):
    B, T, D = x.shape
    N = n_cols
    mod_map = (lambda b, i: (b, 0, 0)) if mod_row is None else (lambda b, i: (mod_row, 0, 0))
    return pl.pallas_call(
        functools.partial(_inproj_kernel, use_pos=use_pos, seg=1024),
        grid=(B, T // tm),
        in_specs=[
            pl.BlockSpec((1, tm, D), lambda b, i: (b, i, 0)),
            pl.BlockSpec(row_tab.shape, lambda b, i: (0, 0, 0)),
            pl.BlockSpec(col_tab.shape, lambda b, i: (0, 0)),
            pl.BlockSpec((1, N_MOD, D), mod_map),
            pl.BlockSpec((1, D), lambda b, i: (0, 0)),
            pl.BlockSpec((D, N), lambda b, i: (0, col_blk), pipeline_mode=pl.Buffered(1)),
            pl.BlockSpec((D, LANE), lambda b, i: (0, 0)),
        ],
        out_specs=[
            pl.BlockSpec((1, tm, N), lambda b, i: (b, i, 0)),
            pl.BlockSpec((1, tm, LANE), lambda b, i: (b, i, 0)),
        ],
        out_shape=[
            jax.ShapeDtypeStruct((B, T, N), BF16),
            jax.ShapeDtypeStruct((B, T, LANE), F32),
        ],
        compiler_params=_params(("parallel", "parallel"), 58),
        name="inproj_x" if use_pos else "inproj_ctx",
    )(x, row_tab, col_tab, mod, g, w, wg)


def _conv_kernel(x_ref, prev_ref, next_ref, w_ref, s_ref, o_ref):
    i = pl.program_id(1)
    x = x_ref[0].astype(F32)
    tc = x.shape[0]
    halo = prev_ref.shape[1]
    taps = w_ref.shape[0]
    prev = jnp.where(i > 0, prev_ref[0].astype(F32), 0.0)
    nxt = jnp.where(i < pl.num_programs(1) - 1, next_ref[0].astype(F32), 0.0)
    ext = jnp.concatenate([prev, x, nxt], axis=0)
    w = w_ref[...]
    acc = None
    for j in range(taps):
        off = halo - taps // 2 + j
        term = ext[off:off + tc] * w[j:j + 1]
        acc = term if acc is None else acc + term
    o_ref[0] = (acc * jax.nn.sigmoid(acc) * s_ref[...]).astype(BF16)


def _conv_silu(p, conv_w, post_scale, *, col0, wcol0, ncol, tc):
    B, T, _ = p.shape
    cb = 1024
    halo = BF16_ROWS
    taps = conv_w.shape[0]
    nhalo = T // halo
    r = tc // halo
    return pl.pallas_call(
        _conv_kernel,
        grid=(B, T // tc, ncol),
        in_specs=[
            pl.BlockSpec((1, tc, cb), lambda b, i, j: (b, i, col0 + j)),
            pl.BlockSpec((1, halo, cb), lambda b, i, j: (b, jnp.maximum(i * r - 1, 0), col0 + j)),
            pl.BlockSpec((1, halo, cb), lambda b, i, j: (b, jnp.minimum((i + 1) * r, nhalo - 1), col0 + j)),
            pl.BlockSpec((taps, cb), lambda b, i, j: (0, wcol0 + j)),
            pl.BlockSpec((1, cb), lambda b, i, j: (0, wcol0 + j)),
        ],
        out_specs=pl.BlockSpec((1, tc, cb), lambda b, i, j: (b, i, j)),
        out_shape=jax.ShapeDtypeStruct((B, T, ncol * cb), BF16),
        compiler_params=_params(("parallel", "parallel", "parallel"), 40),
        name="conv_silu",
    )(p, p, p, conv_w, post_scale)


def _seg_scan(x, seg, op, reverse):
    n = x.shape[-1]
    pos = lax.broadcasted_iota(jnp.int32, x.shape, 1) % seg
    shift = 1
    while shift < seg:
        if reverse:
            other = pltpu.roll(x, n - shift, 1)
            ok = pos < seg - shift
        else:
            other = pltpu.roll(x, shift, 1)
            ok = pos >= shift
        x = jnp.where(ok, op(x, other), x)
        shift *= 2
    return x


def _stack_rows(pieces, n_rows):
    n = pieces[0].shape[-1]
    sub = lax.broadcasted_iota(jnp.int32, (n_rows, n), 0)
    out = jnp.zeros((n_rows, n), F32)
    for r, p in enumerate(pieces):
        out = jnp.where(sub == r, p, out)
    return out


def _gate_scans(g_ref, bias, seg):
    gt = (g_ref[0] + bias).T
    nd = N_DIRS * N_HEADS
    li = gt[0:nd]
    gf = gt[nd:2 * nd]
    lf = -(jnp.maximum(-gf, 0.0) + jnp.log1p(jnp.exp(-jnp.abs(gf))))
    fwd = lax.broadcasted_iota(jnp.int32, (nd, 1), 0) < N_HEADS
    pre = _seg_scan(lf, seg, jnp.add, False)
    suf = _seg_scan(lf, seg, jnp.add, True)
    b = jnp.where(fwd, pre, suf)
    tot = pre + suf - lf
    w = li - b
    pmax = _seg_scan(w, seg, jnp.maximum, False)
    smax = _seg_scan(w, seg, jnp.maximum, True)
    return b, tot, w, jnp.where(fwd, pmax, smax), jnp.maximum(pmax, smax), fwd


def _prep_kernel(gx_ref, gc_ref, bias_ref, rows_ref, cols_ref, ccols_ref, *, chunk):
    H = N_HEADS
    L = chunk
    T = gx_ref.shape[1]
    nc = T // L
    bias = bias_ref[...]

    _, tot_c, w_c, _, wmax_c, _ = _gate_scans(gc_ref, bias, gc_ref.shape[1])
    m_end_c = jnp.maximum(wmax_c, 0.0)
    kws_c = jnp.exp(w_c - m_end_c)
    m0 = (tot_c + m_end_c)[:, :LANE]
    for h in range(H):
        st = _stack_rows([kws_c[h:h + 1], kws_c[H + h:H + h + 1]], LANE)
        ccols_ref[0, h] = st.T

    b, tot, w, cw, wmax, fwd = _gate_scans(gx_ref, bias, L)

    def run(order):
        m_prev = m0
        out = [None] * nc
        for c in order:
            sl = slice(c * L, (c + 1) * L)
            mp = jnp.concatenate([m_prev] * (L // LANE), axis=1)
            m_t = jnp.maximum(mp, cw[:, sl])
            m_end = jnp.maximum(mp, wmax[:, sl])
            out[c] = (m_t, jnp.exp(mp - m_t), jnp.exp(-(b[:, sl] + m_t)),
                      jnp.exp(w[:, sl] - m_end), jnp.exp(mp - m_end))
            m_prev = (tot[:, sl] + m_end)[:, :LANE]
        return [jnp.concatenate([o[q] for o in out], axis=1) for q in range(5)]

    asc = run(range(nc))
    desc = run(range(nc - 1, -1, -1))
    m_t, a_t, e_t, k_t, dec = [jnp.where(fwd, x, y) for x, y in zip(asc, desc)]

    for h in range(H):
        r0, r1 = slice(h, h + 1), slice(H + h, H + h + 1)
        rows = _stack_rows([w[r0], w[r1], dec[r0], dec[r1]], SUBLANE)
        for c in range(nc):
            rows_ref[0, h, c] = rows[:, c * L:(c + 1) * L]
        st = _stack_rows([m_t[r0], a_t[r0], e_t[r0], k_t[r0], m_t[r1], a_t[r1], e_t[r1], k_t[r1]], LANE)
        cols_ref[0, h] = st.T


def _gate_prep(gx, gc, bias, *, chunk):
    B, T, _ = gx.shape
    C = gc.shape[1]
    H = N_HEADS
    nc = T // chunk
    return pl.pallas_call(
        functools.partial(_prep_kernel, chunk=chunk),
        grid=(B,),
        in_specs=[
            pl.BlockSpec((1, T, LANE), lambda b: (b, 0, 0)),
            pl.BlockSpec((1, C, LANE), lambda b: (b, 0, 0)),
            pl.BlockSpec((1, LANE), lambda b: (0, 0)),
        ],
        out_specs=[
            pl.BlockSpec((1, H, nc, SUBLANE, chunk), lambda b: (b, 0, 0, 0, 0)),
            pl.BlockSpec((1, H, T, LANE), lambda b: (b, 0, 0, 0)),
            pl.BlockSpec((1, H, C, LANE), lambda b: (b, 0, 0, 0)),
        ],
        out_shape=[
            jax.ShapeDtypeStruct((B, H, nc, SUBLANE, chunk), F32),
            jax.ShapeDtypeStruct((B, H, T, LANE), F32),
            jax.ShapeDtypeStruct((B, H, C, LANE), F32),
        ],
        compiler_params=_params(("parallel",), 48),
        name="gate_prep",
    )(gx, gc, bias)


_TN = (((0,), (0,)), ((), ()))
_NT = (((1,), (1,)), ((), ()))


def _mlstm_kernel(q_ref, k_ref, v_ref, o_ref, kc_ref, vc_ref, rows_ref, cols_ref, ccols_ref, hg_ref,
                  y_ref, hbuf, sf_ref, sb_ref, *, chunk):
    L = chunk
    T = q_ref.shape[1]
    nc = T // L
    states = (sf_ref, sb_ref)

    def ext(v):
        return jnp.concatenate([v, jnp.ones((v.shape[0], LANE), BF16)], axis=1)

    vc = ext(vc_ref[0])
    kc = kc_ref[0].astype(F32)
    cc = ccols_ref[0, 0]
    for d in range(N_DIRS):
        kw = (kc * cc[:, d:d + 1]).astype(BF16)
        states[d][...] = lax.dot_general(kw, vc, _TN, preferred_element_type=F32)

    row_i = lax.broadcasted_iota(jnp.int32, (L, L), 0)
    col_i = lax.broadcasted_iota(jnp.int32, (L, L), 1)
    causal = (col_i <= row_i, col_i >= row_i)

    def step(c, d):
        t0 = pl.multiple_of(c * L, L)
        q = q_ref[0, pl.ds(t0, L), :]
        k = k_ref[0, pl.ds(t0, L), :]
        v = ext(v_ref[0, pl.ds(t0, L), :])
        dh = q.shape[1]
        cols = cols_ref[0, 0, pl.ds(t0, L), :]
        m_t, a_t, e_t, k_t = (cols[:, 4 * d + j:4 * d + j + 1] for j in range(4))
        rows = rows_ref[0, 0, c]
        w_row = rows[d:d + 1, :]
        dec = rows[N_DIRS + d:N_DIRS + d + 1, :LANE]
        state = states[d][...]

        scores = lax.dot_general(q, k, _NT, preferred_element_type=F32)
        p = (scores * jnp.exp(jnp.where(causal[d], w_row - m_t, -1e30))).astype(BF16)
        aq = (q.astype(F32) * a_t).astype(BF16)
        out = _dot(p, v) + _dot(aq, state.astype(BF16))
        r = 1.0 / jnp.maximum(jnp.abs(out[:, dh:]), e_t)
        h = out[:, :dh] * jnp.concatenate([r] * (dh // LANE), axis=1)

        kw = (k.astype(F32) * k_t).astype(BF16)
        upd = lax.dot_general(kw, v, _TN, preferred_element_type=F32)
        states[d][...] = state * jnp.concatenate([dec] * (state.shape[1] // LANE), axis=1) + upd
        return h

    def finalize(hs, c):
        t0 = pl.multiple_of(c * L, L)
        og = o_ref[0, pl.ds(t0, L), :].astype(F32)
        y_ref[0, pl.ds(t0, L), :] = (_rms(hs, hg_ref[...]) * jax.nn.sigmoid(og)).astype(BF16)

    def first_half(i, carry):
        j = nc - 1 - i
        hbuf[pl.ds(pl.multiple_of(i * L, L), L), :] = step(i, 0)
        hbuf[pl.ds(pl.multiple_of(j * L, L), L), :] = step(j, 1)
        return carry

    def second_half(i, carry):
        j = nc - 1 - i
        finalize(step(i, 0) + hbuf[pl.ds(pl.multiple_of(i * L, L), L), :], i)
        finalize(step(j, 1) + hbuf[pl.ds(pl.multiple_of(j * L, L), L), :], j)
        return carry

    lax.fori_loop(0, nc // 2, first_half, 0)
    lax.fori_loop(nc // 2, nc, second_half, 0)


def _mlstm(qk, p, kc, pc, rows, cols, ccols, head_g, *, chunk, v_blk, o_blk):
    B, T, _ = qk.shape
    C = kc.shape[1]
    H = N_HEADS
    dh = head_g.shape[1] // H
    nc = T // chunk
    assert nc % 2 == 0
    tok = lambda off: pl.BlockSpec((1, T, dh), lambda b, h: (b, 0, off + h))
    ctx = lambda off: pl.BlockSpec((1, C, dh), lambda b, h: (b, 0, off + h))
    return pl.pallas_call(
        functools.partial(_mlstm_kernel, chunk=chunk),
        grid=(B, H),
        in_specs=[
            tok(0), tok(H), tok(v_blk), tok(o_blk), ctx(0), ctx(H),
            pl.BlockSpec((1, 1, nc, SUBLANE, chunk), lambda b, h: (b, h, 0, 0, 0)),
            pl.BlockSpec((1, 1, T, LANE), lambda b, h: (b, h, 0, 0)),
            pl.BlockSpec((1, 1, C, LANE), lambda b, h: (b, h, 0, 0)),
            pl.BlockSpec((1, dh), lambda b, h: (0, h)),
        ],
        out_specs=pl.BlockSpec((1, T, dh), lambda b, h: (b, 0, h)),
        out_shape=jax.ShapeDtypeStruct((B, T, H * dh), BF16),
        scratch_shapes=[
            pltpu.VMEM((T, dh), F32),
            pltpu.VMEM((dh, dh + LANE), F32),
            pltpu.VMEM((dh, dh + LANE), F32),
        ],
        compiler_params=_params(("parallel", "parallel"), 48),
        name="mlstm",
    )(qk, qk, p, p, kc, pc, rows, cols, ccols, head_g)


def _dft_tables(T, gd):
    two_pi = 2.0 * np.pi
    ph = (np.arange(gd)[:, None] * np.arange(gd)[None, :]) % gd
    half = T // 2
    t = np.arange(half, dtype=np.int64)[None, :]
    ph1 = (np.arange(half // DFT_ROWS, dtype=np.int64)[:, None] * DFT_ROWS * t) % T
    ph2 = (np.arange(DFT_ROWS, dtype=np.int64)[:, None] * t) % T
    f = lambda fn, p, n: jnp.asarray(fn(two_pi * p / n).astype(np.float32))
    return (f(np.cos, ph, gd), f(np.sin, ph, gd),
            f(np.cos, ph1, T), f(np.sin, ph1, T), f(np.cos, ph2, T), f(np.sin, ph2, T))


def _fourier_kernel(f_ref, cc_ref, sc_ref, f1c_ref, f1s_ref, f2c_ref, f2s_ref, y_ref, ch, sh, *, scale):
    T = f_ref.shape[1]
    half = T // 2
    R = f2c_ref.shape[0]
    r = 256
    nb = half // r

    @pl.when((pl.program_id(0) == 0) & (pl.program_id(1) == 0))
    def _():
        def gen(k1, carry):
            c1 = f1c_ref[pl.ds(k1, 1), :]
            s1 = f1s_ref[pl.ds(k1, 1), :]
            rows = pl.ds(pl.multiple_of(k1 * R, R), R)
            ch[rows, :] = (c1 * f2c_ref[...] - s1 * f2s_ref[...]).astype(BF16)
            sh[rows, :] = (s1 * f2c_ref[...] + c1 * f2s_ref[...]).astype(BF16)
            return carry
        lax.fori_loop(0, half // R, gen, 0)

    ia = lax.broadcasted_iota(jnp.int32, (r, r), 0)
    ic = lax.broadcasted_iota(jnp.int32, (r, r), 1)
    flip = jnp.where((ia + ic == r) & (ia > 0), 1.0, 0.0).astype(BF16)
    first = lax.broadcasted_iota(jnp.int32, (r, 1), 0) == 0

    def reversed_shifted(block, row0):
        out = []
        for i in range(nb):
            head = row0 if i == 0 else block(nb - i)[0:1].astype(F32)
            out.append(jnp.where(first, head, _dot(flip, block(nb - 1 - i))))
        return jnp.concatenate(out, axis=0)

    cc = cc_ref[...]
    sc = sc_ref[...]
    f_lo = f_ref[0, 0:half, :].astype(F32)
    zero_row = jnp.zeros((1, f_lo.shape[1]), F32)
    f_rev = reversed_shifted(lambda j: f_ref[0, half + j * r:half + (j + 1) * r, :], zero_row)
    pe = _dot((f_lo + f_rev).astype(BF16), cc).astype(BF16)
    qo = _dot((f_lo - f_rev).astype(BF16), sc).astype(BF16)

    k_odd = (lax.broadcasted_iota(jnp.int32, (half, 1), 0) & 1) == 1
    f_mid = f_ref[0, half:half + BF16_ROWS, :]
    p_mid = _dot(f_mid, cc)[0:1]
    t_sign = jnp.where((lax.broadcasted_iota(jnp.int32, (SUBLANE, T), 1) & 1) == 1, -1.0, 1.0).astype(BF16)
    alt = _dot(t_sign, f_ref[0])[0:1]
    y_mid = _dot(jnp.broadcast_to(alt, (SUBLANE, alt.shape[1])).astype(BF16), cc)[0:1] * scale

    cp = _dot(ch[...], pe) + jnp.where(k_odd, -p_mid, p_mid)
    sq = _dot(sh[...], qo)
    y_ref[0, 0:half, :] = ((cp - sq) * scale).astype(BF16)
    u = ((cp + sq) * scale).astype(BF16)
    y_ref[0, half:, :] = reversed_shifted(lambda j: u[j * r:(j + 1) * r], y_mid).astype(BF16)


def _fourier_mix(p, tables, *, gd, n_groups):
    B, T, _ = p.shape
    half = T // 2
    assert half % 256 == 0 and half % DFT_ROWS == 0
    whole = lambda a: pl.BlockSpec(a.shape, lambda b, g: (0, 0), pipeline_mode=pl.Buffered(1))
    cc, sc = (t.astype(BF16) for t in tables[:2])
    return pl.pallas_call(
        functools.partial(_fourier_kernel, scale=float((T * gd) ** -0.5)),
        grid=(B, n_groups),
        in_specs=[pl.BlockSpec((1, T, gd), lambda b, g: (b, 0, g)), whole(cc), whole(sc)]
        + [whole(t) for t in tables[2:]],
        out_specs=pl.BlockSpec((1, T, gd), lambda b, g: (b, 0, g)),
        out_shape=jax.ShapeDtypeStruct((B, T, n_groups * gd), BF16),
        scratch_shapes=[pltpu.VMEM((half, half), BF16), pltpu.VMEM((half, half), BF16)],
        compiler_params=_params(("arbitrary", "arbitrary"), 52),
        name="fourier_mix",
    )(p, cc, sc, *tables[2:])


def _outproj_kernel(yf_ref, ym_ref, x_ref, row_ref, col_ref, mod_ref, g_ref, w_ref, x2_ref, hm_ref):
    fw = yf_ref.shape[2]
    x = _add_pos(x_ref[0], row_ref, col_ref, pl.program_id(1))
    mix = _dot(yf_ref[0], w_ref[0:fw, :]) + _dot(ym_ref[0], w_ref[fw:, :])
    x2 = x + mod_ref[0, 2:3, :] * mix
    x2_ref[0] = x2
    y = _rms(x2, g_ref[...])
    hm_ref[0] = (y * (1.0 + mod_ref[0, 4:5, :]) + mod_ref[0, 3:4, :]).astype(BF16)


def _outproj(yf, ym, x, row_tab, col_tab, mod, g, w, *, tm):
    B, T, D = x.shape
    fw, mw = yf.shape[2], ym.shape[2]
    return pl.pallas_call(
        _outproj_kernel,
        grid=(B, T // tm),
        in_specs=[
            pl.BlockSpec((1, tm, fw), lambda b, i: (b, i, 0)),
            pl.BlockSpec((1, tm, mw), lambda b, i: (b, i, 0)),
            pl.BlockSpec((1, tm, D), lambda b, i: (b, i, 0)),
            pl.BlockSpec(row_tab.shape, lambda b, i: (0, 0, 0)),
            pl.BlockSpec(col_tab.shape, lambda b, i: (0, 0)),
            pl.BlockSpec((1, N_MOD, D), lambda b, i: (b, 0, 0)),
            pl.BlockSpec((1, D), lambda b, i: (0, 0)),
            pl.BlockSpec((fw + mw, D), lambda b, i: (0, 0), pipeline_mode=pl.Buffered(1)),
        ],
        out_specs=[
            pl.BlockSpec((1, tm, D), lambda b, i: (b, i, 0)),
            pl.BlockSpec((1, tm, D), lambda b, i: (b, i, 0)),
        ],
        out_shape=[
            jax.ShapeDtypeStruct((B, T, D), F32),
            jax.ShapeDtypeStruct((B, T, D), BF16),
        ],
        compiler_params=_params(("parallel", "parallel"), 52),
        name="outproj",
    )(yf, ym, x, row_tab, col_tab, mod, g, w)


def _mlp_kernel(hm_ref, w1_ref, w2_ref, x2_ref, mod_ref, g_ref, o_ref, acc):
    f = pl.program_id(2)

    @pl.when(f == 0)
    def _():
        acc[...] = jnp.zeros_like(acc)

    hid = jnp.square(jnp.maximum(_dot(hm_ref[0], w1_ref[...]), 0.0)).astype(BF16)
    acc[...] += _dot(hid, w2_ref[...])

    @pl.when(f == pl.num_programs(2) - 1)
    def _():
        o_ref[0] = _rms(x2_ref[0] + mod_ref[0, 5:6, :] * acc[...], g_ref[...])


def _mlp(hm, w1, w2, x2, mod, g, *, tm, tf):
    B, T, D = x2.shape
    dff = w1.shape[1]
    return pl.pallas_call(
        _mlp_kernel,
        grid=(B, T // tm, dff // tf),
        in_specs=[
            pl.BlockSpec((1, tm, D), lambda b, i, f: (b, i, 0)),
            pl.BlockSpec((D, tf), lambda b, i, f: (0, f)),
            pl.BlockSpec((tf, D), lambda b, i, f: (f, 0)),
            pl.BlockSpec((1, tm, D), lambda b, i, f: (b, i, 0)),
            pl.BlockSpec((1, N_MOD, D), lambda b, i, f: (b, 0, 0)),
            pl.BlockSpec((1, D), lambda b, i, f: (0, 0)),
        ],
        out_specs=pl.BlockSpec((1, tm, D), lambda b, i, f: (b, i, 0)),
        out_shape=jax.ShapeDtypeStruct((B, T, D), F32),
        scratch_shapes=[pltpu.VMEM((tm, D), F32)],
        compiler_params=_params(("parallel", "parallel", "arbitrary"), 56),
        name="mlp",
    )(hm, w1, w2, x2, mod, g)


def kernel(x, c, ctx, c_ctx, w_mod, b_mod, g_mix, g_mlp, w_in, conv_w, gate_b, head_g, w_out,
           w_mlp1, w_mlp2, g_final):
    assert w_mod.shape[0] == 1, "single-layer block"
    B, T, D = x.shape
    H = N_HEADS
    n_gate = N_DIRS * 2 * H
    mw = conv_w.shape[2] // 2
    fw = w_in.shape[2] - 4 * mw - n_gate
    gd = fw // N_GROUPS
    dh = mw // H
    assert T % GRID_W == 0 and fw % 1024 == 0 and mw % 1024 == 0 and dh % LANE == 0

    mod = _modulation(c, c_ctx, w_mod[0], b_mod[0])
    row_tab, col_tab = _pos_tables(T // GRID_W, D)
    g_mix2, g_mlp2, g_fin2 = (g.reshape(1, D) for g in (g_mix[0], g_mlp[0], g_final))

    perm = np.array([d * 2 * H + gate * H + h for gate in range(2) for d in range(N_DIRS) for h in range(H)])
    w_all = w_in[0].astype(BF16)
    w_gate = jnp.pad(w_in[0, :, fw + 4 * mw:][:, perm], ((0, 0), (0, LANE - n_gate))).astype(BF16)
    bias = jnp.pad(gate_b[0].reshape(n_gate)[perm], (0, LANE - n_gate)).reshape(1, LANE)

    p, gx = _inproj(x, row_tab, col_tab, mod, None, g_mix2, w_all, w_gate, use_pos=True, tm=512,
                    n_cols=fw + 4 * mw, col_blk=0)
    assert (fw + mw) % (2 * mw) == 0
    pc, gc = _inproj(ctx, row_tab, col_tab, mod, B, g_mix2, w_all, w_gate, use_pos=False, tm=ctx.shape[1],
                     n_cols=2 * mw, col_blk=(fw + mw) // (2 * mw))

    post = jnp.concatenate([jnp.full((1, mw), dh ** -0.5, F32), jnp.ones((1, mw), F32)], axis=1)
    qk = _conv_silu(p, conv_w[0], post, col0=fw // 1024, wcol0=0, ncol=2 * mw // 1024, tc=512)
    kc = _conv_silu(pc, conv_w[0], post, col0=0, wcol0=mw // 1024, ncol=mw // 1024, tc=ctx.shape[1])

    rows, cols, ccols = _gate_prep(gx, gc, bias, chunk=MLSTM_CHUNK)
    ym = _mlstm(qk, p, kc, pc, rows, cols, ccols, head_g[0].reshape(1, mw), chunk=MLSTM_CHUNK,
                v_blk=(fw + 2 * mw) // dh, o_blk=(fw + 3 * mw) // dh)

    yf = _fourier_mix(p, _dft_tables(T, gd), gd=gd, n_groups=N_GROUPS)

    x2, hm = _outproj(yf, ym, x, row_tab, col_tab, mod, g_mlp2, w_out[0].astype(BF16), tm=512)
    return _mlp(hm, w_mlp1[0].astype(BF16), w_mlp2[0].astype(BF16), x2, mod, g_fin2, tm=512, tf=1024)
```

```python
import functools

import numpy as np
import jax
import jax.numpy as jnp
from jax import lax
from jax.experimental import pallas as pl
from jax.experimental.pallas import tpu as pltpu

F32 = jnp.float32
BF16 = jnp.bfloat16

GRID_W = 64
N_HEADS = 4
N_DIRS = 2
N_GROUPS = 4
N_MOD = 6
EPS = 1e-6
POS_BASE = 10000.0

LANE = 128
SUBLANE = 8
BF16_ROWS = 16
MIB = 1024 * 1024
V7X_VMEM_BYTES = 64 * MIB

MLSTM_CHUNK = 256
DFT_ROWS = 64
LOOP_UNROLL = 4


def _params(semantics, vmem_mib):
    assert vmem_mib * MIB < V7X_VMEM_BYTES
    return pltpu.CompilerParams(dimension_semantics=semantics, vmem_limit_bytes=vmem_mib * MIB)


def _dot(a, b):
    return jnp.dot(a, b, preferred_element_type=F32)


def _rms(x, gain):
    return x * lax.rsqrt(jnp.mean(x * x, axis=-1, keepdims=True) + EPS) * gain


def _mod_kernel(s_ref, w_ref, b_ref, o_ref):
    s = s_ref[...]
    a = (s * jax.nn.sigmoid(s)).astype(BF16)
    o_ref[...] = _dot(a, w_ref[...].astype(BF16)) + b_ref[...]


def _modulation(c, c_ctx, w_mod, b_mod):
    B, D = c.shape
    N = w_mod.shape[1]
    rows = -(-(B + 1) // SUBLANE) * SUBLANE
    s = jnp.zeros((rows, D), F32).at[:B].set(c).at[B].set(c_ctx)
    tn = 1024
    out = pl.pallas_call(
        _mod_kernel,
        grid=(N // tn,),
        in_specs=[
            pl.BlockSpec((rows, D), lambda j: (0, 0)),
            pl.BlockSpec((D, tn), lambda j: (0, j)),
            pl.BlockSpec((1, tn), lambda j: (0, j)),
        ],
        out_specs=pl.BlockSpec((rows, tn), lambda j: (0, j)),
        out_shape=jax.ShapeDtypeStruct((rows, N), F32),
        compiler_params=_params(("arbitrary",), 40),
        name="modulation",
    )(s, w_mod, b_mod.reshape(1, N))
    return out.reshape(rows, N_MOD, D)


def _pos_tables(rows, d_model):
    quarter = d_model // 4
    omega = 1.0 / (POS_BASE ** (np.arange(quarter, dtype=np.float64) / quarter))

    def tab(n):
        a = np.arange(n, dtype=np.float64)[:, None] * omega
        return np.concatenate([np.sin(a), np.cos(a)], -1).astype(np.float32)

    return jnp.asarray(tab(rows)).reshape(rows, 1, 2 * quarter), jnp.asarray(tab(GRID_W))


def _add_pos(x, row_ref, col_ref, tile_idx):
    tm, d = x.shape
    g = tm // GRID_W
    rows = row_ref[pl.ds(tile_idx * g, g)]
    pos = jnp.concatenate(
        [jnp.broadcast_to(rows, (g, GRID_W, d // 2)),
         jnp.broadcast_to(col_ref[...][None], (g, GRID_W, d // 2))], axis=-1)
    return x + pos.reshape(tm, d)


def _inproj_kernel(x_ref, row_ref, col_ref, mod_ref, g_ref, w_ref, wg_ref, p_ref, gate_ref, *, use_pos, seg,
                   sigmoid_seg):
    x = x_ref[0]
    if use_pos:
        x = _add_pos(x, row_ref, col_ref, pl.program_id(1))
    y = _rms(x, g_ref[...])
    h = (y * (1.0 + mod_ref[0, 1:2, :]) + mod_ref[0, 0:1, :]).astype(BF16)
    for s in range(w_ref.shape[1] // seg):
        z = _dot(h, w_ref[:, s * seg:(s + 1) * seg])
        p_ref[0, :, s * seg:(s + 1) * seg] = (jax.nn.sigmoid(z) if s == sigmoid_seg else z).astype(BF16)
    gate_ref[0] = _dot(h, wg_ref[...])


def _inproj(x, row_tab, col_tab, mod, mod_row, g, w, wg, *, use_pos, tm, n_cols, col_blk, sigmoid_seg):
    B, T, D = x.shape
    N = n_cols
    mod_map = (lambda b, i: (b, 0, 0)) if mod_row is None else (lambda b, i: (mod_row, 0, 0))
    return pl.pallas_call(
        functools.partial(_inproj_kernel, use_pos=use_pos, seg=1024, sigmoid_seg=sigmoid_seg),
        grid=(B, T // tm),
        in_specs=[
            pl.BlockSpec((1, tm, D), lambda b, i: (b, i, 0)),
            pl.BlockSpec(row_tab.shape, lambda b, i: (0, 0, 0)),
            pl.BlockSpec(col_tab.shape, lambda b, i: (0, 0)),
            pl.BlockSpec((1, N_MOD, D), mod_map),
            pl.BlockSpec((1, D), lambda b, i: (0, 0)),
            pl.BlockSpec((D, N), lambda b, i: (0, col_blk), pipeline_mode=pl.Buffered(1)),
            pl.BlockSpec((D, LANE), lambda b, i: (0, 0)),
        ],
        out_specs=[
            pl.BlockSpec((1, tm, N), lambda b, i: (b, i, 0)),
            pl.BlockSpec((1, tm, LANE), lambda b, i: (b, i, 0)),
        ],
        out_shape=[
            jax.ShapeDtypeStruct((B, T, N), BF16),
            jax.ShapeDtypeStruct((B, T, LANE), F32),
        ],
        compiler_params=_params(("parallel", "parallel"), 58),
        name="inproj_x" if use_pos else "inproj_ctx",
    )(x, row_tab, col_tab, mod, g, w, wg)


def _conv_kernel(x_ref, prev_ref, next_ref, w_ref, s_ref, o_ref, ext, *, transposed):
    i = pl.program_id(1)
    tc = x_ref.shape[1]
    halo = prev_ref.shape[1]
    taps = w_ref.shape[0]
    ext[0:halo] = jnp.where(i > 0, prev_ref[0].astype(F32), 0.0)
    ext[halo:halo + tc] = x_ref[0].astype(F32)
    ext[halo + tc:] = jnp.where(i < pl.num_programs(1) - 1, next_ref[0].astype(F32), 0.0)
    acc = None
    for j in range(taps):
        term = ext[pl.ds(halo - taps // 2 + j, tc), :] * w_ref[j:j + 1, :]
        acc = term if acc is None else acc + term
    y = acc * jax.nn.sigmoid(acc) * s_ref[...]
    n, width = o_ref.shape[1], o_ref.shape[3]
    if transposed:
        yt = y.T
        for c in range(n):
            o_ref[0, c] = yt[:, c * width:(c + 1) * width].astype(BF16)
    else:
        for h in range(n):
            o_ref[0, h] = y[:, h * width:(h + 1) * width].astype(BF16)


def _conv_silu(p, conv_w, post_scale, *, col_blk, wcol_blk, tc, transposed, width):
    B, T, _ = p.shape
    cb = 1024
    halo = BF16_ROWS
    taps = conv_w.shape[0]
    nhalo = T // halo
    r = tc // halo
    if transposed:
        out_shape, out_block = (B, T // width, cb, width), (1, tc // width, cb, width)
        out_map = lambda b, i: (b, i, 0, 0)
    else:
        out_shape, out_block = (B, cb // width, T, width), (1, cb // width, tc, width)
        out_map = lambda b, i: (b, 0, i, 0)
    return pl.pallas_call(
        functools.partial(_conv_kernel, transposed=transposed),
        grid=(B, T // tc),
        in_specs=[
            pl.BlockSpec((1, tc, cb), lambda b, i: (b, i, col_blk)),
            pl.BlockSpec((1, halo, cb), lambda b, i: (b, jnp.maximum(i * r - 1, 0), col_blk)),
            pl.BlockSpec((1, halo, cb), lambda b, i: (b, jnp.minimum((i + 1) * r, nhalo - 1), col_blk)),
            pl.BlockSpec((taps, cb), lambda b, i: (0, wcol_blk)),
            pl.BlockSpec((1, cb), lambda b, i: (0, wcol_blk)),
        ],
        out_specs=pl.BlockSpec(out_block, out_map),
        out_shape=jax.ShapeDtypeStruct(out_shape, BF16),
        scratch_shapes=[pltpu.VMEM((tc + 2 * halo, cb), F32)],
        compiler_params=_params(("parallel", "parallel"), 40),
        name="conv_silu_t" if transposed else "conv_silu",
    )(p, p, p, conv_w, post_scale)


def _seg_scan(x, seg, op, reverse):
    n = x.shape[-1]
    pos = lax.broadcasted_iota(jnp.int32, x.shape, 1) % seg
    shift = 1
    while shift < seg:
        if reverse:
            other = pltpu.roll(x, n - shift, 1)
            ok = pos < seg - shift
        else:
            other = pltpu.roll(x, shift, 1)
            ok = pos >= shift
        x = jnp.where(ok, op(x, other), x)
        shift *= 2
    return x


def _stack_rows(pieces, n_rows):
    n = pieces[0].shape[-1]
    sub = lax.broadcasted_iota(jnp.int32, (n_rows, n), 0)
    out = jnp.zeros((n_rows, n), F32)
    for r, p in enumerate(pieces):
        out = jnp.where(sub == r, p, out)
    return out


def _gate_scans(g_ref, bias, seg):
    gt = (g_ref[0] + bias).T
    nd = N_DIRS * N_HEADS
    li = gt[0:nd]
    gf = gt[nd:2 * nd]
    lf = -(jnp.maximum(-gf, 0.0) + jnp.log1p(jnp.exp(-jnp.abs(gf))))
    fwd = lax.broadcasted_iota(jnp.int32, (nd, 1), 0) < N_HEADS
    pre = _seg_scan(lf, seg, jnp.add, False)
    suf = _seg_scan(lf, seg, jnp.add, True)
    b = jnp.where(fwd, pre, suf)
    tot = pre + suf - lf
    w = li - b
    pmax = _seg_scan(w, seg, jnp.maximum, False)
    smax = _seg_scan(w, seg, jnp.maximum, True)
    return b, tot, w, jnp.where(fwd, pmax, smax), jnp.maximum(pmax, smax), fwd


def _prep_kernel(gx_ref, gc_ref, bias_ref, rows_ref, cols_ref, crows_ref, *, chunk):
    H = N_HEADS
    L = chunk
    T = gx_ref.shape[1]
    nc = T // L
    bias = bias_ref[...]

    _, tot_c, w_c, _, wmax_c, _ = _gate_scans(gc_ref, bias, gc_ref.shape[1])
    m_end_c = jnp.maximum(wmax_c, 0.0)
    kws_c = jnp.exp(w_c - m_end_c)
    m0 = (tot_c + m_end_c)[:, :LANE]
    for h in range(H):
        crows_ref[0, h] = _stack_rows([kws_c[h:h + 1], kws_c[H + h:H + h + 1]], SUBLANE)

    b, tot, w, cw, wmax, fwd = _gate_scans(gx_ref, bias, L)

    def run(order):
        m_prev = m0
        out = [None] * nc
        for c in order:
            sl = slice(c * L, (c + 1) * L)
            mp = jnp.concatenate([m_prev] * (L // LANE), axis=1)
            m_t = jnp.maximum(mp, cw[:, sl])
            m_end = jnp.maximum(mp, wmax[:, sl])
            out[c] = (m_t, jnp.exp(mp - m_t), jnp.exp(-(b[:, sl] + m_t)),
                      jnp.exp(w[:, sl] - m_end), jnp.exp(mp - m_end))
            m_prev = (tot[:, sl] + m_end)[:, :LANE]
        return [jnp.concatenate([o[q] for o in out], axis=1) for q in range(5)]

    asc = run(range(nc))
    desc = run(range(nc - 1, -1, -1))
    m_t, a_t, e_t, k_t, dec = [jnp.where(fwd, x, y) for x, y in zip(asc, desc)]

    for h in range(H):
        r0, r1 = slice(h, h + 1), slice(H + h, H + h + 1)
        rows = _stack_rows([w[r0], w[r1], dec[r0], dec[r1], k_t[r0], k_t[r1]], SUBLANE)
        for c in range(nc):
            rows_ref[0, h, c] = rows[:, c * L:(c + 1) * L]
        cols_ref[0, h] = _stack_rows([m_t[r0], a_t[r0], e_t[r0], m_t[r1], a_t[r1], e_t[r1]], LANE).T


def _gate_prep(gx, gc, bias, *, chunk):
    B, T, _ = gx.shape
    C = gc.shape[1]
    H = N_HEADS
    nc = T // chunk
    return pl.pallas_call(
        functools.partial(_prep_kernel, chunk=chunk),
        grid=(B,),
        in_specs=[
            pl.BlockSpec((1, T, LANE), lambda b: (b, 0, 0)),
            pl.BlockSpec((1, C, LANE), lambda b: (b, 0, 0)),
            pl.BlockSpec((1, LANE), lambda b: (0, 0)),
        ],
        out_specs=[
            pl.BlockSpec((1, H, nc, SUBLANE, chunk), lambda b: (b, 0, 0, 0, 0)),
            pl.BlockSpec((1, H, T, LANE), lambda b: (b, 0, 0, 0)),
            pl.BlockSpec((1, H, SUBLANE, C), lambda b: (b, 0, 0, 0)),
        ],
        out_shape=[
            jax.ShapeDtypeStruct((B, H, nc, SUBLANE, chunk), F32),
            jax.ShapeDtypeStruct((B, H, T, LANE), F32),
            jax.ShapeDtypeStruct((B, H, SUBLANE, C), F32),
        ],
        compiler_params=_params(("parallel",), 48),
        name="gate_prep",
    )(gx, gc, bias)


def _mlstm_kernel(q_ref, kt_ref, v_ref, o_ref, kct_ref, vc_ref, rows_ref, cols_ref, crows_ref, hg_ref,
                  y_ref, hbuf, sf_ref, sb_ref):
    nc, dh, L = kt_ref.shape[1:]
    states = (sf_ref, sb_ref)

    def ext(v):
        return jnp.concatenate([v, jnp.ones((v.shape[0], LANE), BF16)], axis=1)

    vc = ext(vc_ref[0])
    for d in range(N_DIRS):
        kw = kct_ref[0, 0] * crows_ref[0, 0, d:d + 1, :].astype(BF16)
        states[d][...] = _dot(kw, vc)

    row_i = lax.broadcasted_iota(jnp.int32, (L, L), 0)
    col_i = lax.broadcasted_iota(jnp.int32, (L, L), 1)
    causal = (col_i <= row_i, col_i >= row_i)

    def step(c, d):
        t0 = pl.multiple_of(c * L, L)
        q = q_ref[0, 0, pl.ds(t0, L), :]
        kt = kt_ref[0, c]
        v = ext(v_ref[0, pl.ds(t0, L), :])
        cols = cols_ref[0, 0, pl.ds(t0, L), :]
        m_t, a_t, e_t = (cols[:, 3 * d + j:3 * d + j + 1] for j in range(3))
        rows = rows_ref[0, 0, c]
        w_row = rows[d:d + 1, :]
        dec = rows[N_DIRS + d:N_DIRS + d + 1, :LANE]
        kw_row = rows[2 * N_DIRS + d:2 * N_DIRS + d + 1, :]
        state = states[d][...]

        p = (_dot(q, kt) * jnp.exp(jnp.where(causal[d], w_row - m_t, -1e30))).astype(BF16)
        aq = q * a_t.astype(BF16)
        out = _dot(p, v) + _dot(aq, state.astype(BF16))
        r = 1.0 / jnp.maximum(jnp.abs(out[:, dh:]), e_t)
        h = out[:, :dh] * jnp.concatenate([r] * (dh // LANE), axis=1)

        upd = _dot(kt * kw_row.astype(BF16), v)
        states[d][...] = state * jnp.concatenate([dec] * (state.shape[1] // LANE), axis=1) + upd
        return h

    def finalize(hs, c):
        t0 = pl.multiple_of(c * L, L)
        gate = o_ref[0, pl.ds(t0, L), :].astype(F32)
        y_ref[0, pl.ds(t0, L), :] = (_rms(hs, hg_ref[...]) * gate).astype(BF16)

    def first_half(i, carry):
        j = nc - 1 - i
        hbuf[pl.ds(pl.multiple_of(i * L, L), L), :] = step(i, 0)
        hbuf[pl.ds(pl.multiple_of(j * L, L), L), :] = step(j, 1)
        return carry

    def second_half(i, carry):
        j = nc - 1 - i
        finalize(step(i, 0) + hbuf[pl.ds(pl.multiple_of(i * L, L), L), :], i)
        finalize(step(j, 1) + hbuf[pl.ds(pl.multiple_of(j * L, L), L), :], j)
        return carry

    lax.fori_loop(0, nc // 2, first_half, 0, unroll=LOOP_UNROLL)
    lax.fori_loop(nc // 2, nc, second_half, 0, unroll=LOOP_UNROLL)


def _mlstm(q, kt, p, kct, pc, rows, cols, crows, head_g, *, v_blk, o_blk, vc_blk):
    B, H, T, dh = q.shape
    nc, _, L = kt.shape[1:]
    C = kct.shape[3]
    assert nc % 2 == 0
    tok = lambda off: pl.BlockSpec((1, T, dh), lambda b, h: (b, 0, off + h))
    return pl.pallas_call(
        _mlstm_kernel,
        grid=(B, H),
        in_specs=[
            pl.BlockSpec((1, 1, T, dh), lambda b, h: (b, h, 0, 0)),
            pl.BlockSpec((1, nc, dh, L), lambda b, h: (b, 0, h, 0)),
            tok(v_blk), tok(o_blk),
            pl.BlockSpec((1, 1, dh, C), lambda b, h: (b, 0, h, 0)),
            pl.BlockSpec((1, C, dh), lambda b, h: (b, 0, vc_blk + h)),
            pl.BlockSpec((1, 1, nc, SUBLANE, L), lambda b, h: (b, h, 0, 0, 0)),
            pl.BlockSpec((1, 1, T, LANE), lambda b, h: (b, h, 0, 0)),
            pl.BlockSpec((1, 1, SUBLANE, C), lambda b, h: (b, h, 0, 0)),
            pl.BlockSpec((1, dh), lambda b, h: (0, h)),
        ],
        out_specs=pl.BlockSpec((1, T, dh), lambda b, h: (b, 0, h)),
        out_shape=jax.ShapeDtypeStruct((B, T, H * dh), BF16),
        scratch_shapes=[
            pltpu.VMEM((T, dh), F32),
            pltpu.VMEM((dh, dh + LANE), F32),
            pltpu.VMEM((dh, dh + LANE), F32),
        ],
        compiler_params=_params(("parallel", "parallel"), 48),
        name="mlstm",
    )(q, kt, p, p, kct, pc, rows, cols, crows, head_g)


def _dft_tables(T, gd):
    two_pi = 2.0 * np.pi
    ph = (np.arange(gd)[:, None] * np.arange(gd)[None, :]) % gd
    half = T // 2
    t = np.arange(half, dtype=np.int64)[None, :]
    ph1 = (np.arange(half // DFT_ROWS, dtype=np.int64)[:, None] * DFT_ROWS * t) % T
    ph2 = (np.arange(DFT_ROWS, dtype=np.int64)[:, None] * t) % T
    f = lambda fn, p, n: jnp.asarray(fn(two_pi * p / n).astype(np.float32))
    return (f(np.cos, ph, gd), f(np.sin, ph, gd),
            f(np.cos, ph1, T), f(np.sin, ph1, T), f(np.cos, ph2, T), f(np.sin, ph2, T))


def _fourier_kernel(f_ref, cc_ref, sc_ref, f1c_ref, f1s_ref, f2c_ref, f2s_ref, y_ref, ch, sh, *, scale):
    T = f_ref.shape[1]
    half = T // 2
    R = f2c_ref.shape[0]
    r = 256
    nb = half // r

    @pl.when((pl.program_id(0) == 0) & (pl.program_id(1) == 0))
    def _():
        def gen(k1, carry):
            c1 = f1c_ref[pl.ds(k1, 1), :]
            s1 = f1s_ref[pl.ds(k1, 1), :]
            rows = pl.ds(pl.multiple_of(k1 * R, R), R)
            ch[rows, :] = (c1 * f2c_ref[...] - s1 * f2s_ref[...]).astype(BF16)
            sh[rows, :] = (s1 * f2c_ref[...] + c1 * f2s_ref[...]).astype(BF16)
            return carry
        lax.fori_loop(0, half // R, gen, 0)

    ia = lax.broadcasted_iota(jnp.int32, (r, r), 0)
    ic = lax.broadcasted_iota(jnp.int32, (r, r), 1)
    flip = jnp.where((ia + ic == r) & (ia > 0), 1.0, 0.0).astype(BF16)
    first = lax.broadcasted_iota(jnp.int32, (r, 1), 0) == 0

    def reversed_shifted(block, row0):
        out = []
        for i in range(nb):
            head = row0 if i == 0 else block(nb - i)[0:1].astype(F32)
            out.append(jnp.where(first, head, _dot(flip, block(nb - 1 - i))))
        return jnp.concatenate(out, axis=0)

    cc = cc_ref[...]
    sc = sc_ref[...]
    f_lo = f_ref[0, 0:half, :].astype(F32)
    zero_row = jnp.zeros((1, f_lo.shape[1]), F32)
    f_rev = reversed_shifted(lambda j: f_ref[0, half + j * r:half + (j + 1) * r, :], zero_row)
    pe = _dot((f_lo + f_rev).astype(BF16), cc).astype(BF16)
    qo = _dot((f_lo - f_rev).astype(BF16), sc).astype(BF16)

    k_odd = (lax.broadcasted_iota(jnp.int32, (half, 1), 0) & 1) == 1
    f_mid = f_ref[0, half:half + BF16_ROWS, :]
    p_mid = _dot(f_mid, cc)[0:1]
    t_sign = jnp.where((lax.broadcasted_iota(jnp.int32, (SUBLANE, T), 1) & 1) == 1, -1.0, 1.0).astype(BF16)
    alt = _dot(t_sign, f_ref[0])[0:1]
    y_mid = _dot(jnp.broadcast_to(alt, (SUBLANE, alt.shape[1])).astype(BF16), cc)[0:1] * scale

    cp = _dot(ch[...], pe) + jnp.where(k_odd, -p_mid, p_mid)
    sq = _dot(sh[...], qo)
    y_ref[0, 0:half, :] = ((cp - sq) * scale).astype(BF16)
    u = ((cp + sq) * scale).astype(BF16)
    y_ref[0, half:, :] = reversed_shifted(lambda j: u[j * r:(j + 1) * r], y_mid).astype(BF16)


def _fourier_mix(p, tables, *, gd, n_groups):
    B, T, _ = p.shape
    half = T // 2
    assert half % 256 == 0 and half % DFT_ROWS == 0
    whole = lambda a: pl.BlockSpec(a.shape, lambda b, g: (0, 0), pipeline_mode=pl.Buffered(1))
    cc, sc = (t.astype(BF16) for t in tables[:2])
    return pl.pallas_call(
        functools.partial(_fourier_kernel, scale=float((T * gd) ** -0.5)),
        grid=(B, n_groups),
        in_specs=[pl.BlockSpec((1, T, gd), lambda b, g: (b, 0, g)), whole(cc), whole(sc)]
        + [whole(t) for t in tables[2:]],
        out_specs=pl.BlockSpec((1, T, gd), lambda b, g: (b, 0, g)),
        out_shape=jax.ShapeDtypeStruct((B, T, n_groups * gd), BF16),
        scratch_shapes=[pltpu.VMEM((half, half), BF16), pltpu.VMEM((half, half), BF16)],
        compiler_params=_params(("arbitrary", "arbitrary"), 52),
        name="fourier_mix",
    )(p, cc, sc, *tables[2:])


def _outproj_kernel(yf_ref, ym_ref, x_ref, row_ref, col_ref, mod_ref, g_ref, w_ref, x2_ref, hm_ref):
    fw = yf_ref.shape[2]
    x = _add_pos(x_ref[0], row_ref, col_ref, pl.program_id(1))
    mix = _dot(yf_ref[0], w_ref[0:fw, :]) + _dot(ym_ref[0], w_ref[fw:, :])
    x2 = x + mod_ref[0, 2:3, :] * mix
    x2_ref[0] = x2
    y = _rms(x2, g_ref[...])
    hm_ref[0] = (y * (1.0 + mod_ref[0, 4:5, :]) + mod_ref[0, 3:4, :]).astype(BF16)


def _outproj(yf, ym, x, row_tab, col_tab, mod, g, w, *, tm):
    B, T, D = x.shape
    fw, mw = yf.shape[2], ym.shape[2]
    return pl.pallas_call(
        _outproj_kernel,
        grid=(B, T // tm),
        in_specs=[
            pl.BlockSpec((1, tm, fw), lambda b, i: (b, i, 0)),
            pl.BlockSpec((1, tm, mw), lambda b, i: (b, i, 0)),
            pl.BlockSpec((1, tm, D), lambda b, i: (b, i, 0)),
            pl.BlockSpec(row_tab.shape, lambda b, i: (0, 0, 0)),
            pl.BlockSpec(col_tab.shape, lambda b, i: (0, 0)),
            pl.BlockSpec((1, N_MOD, D), lambda b, i: (b, 0, 0)),
            pl.BlockSpec((1, D), lambda b, i: (0, 0)),
            pl.BlockSpec((fw + mw, D), lambda b, i: (0, 0), pipeline_mode=pl.Buffered(1)),
        ],
        out_specs=[
            pl.BlockSpec((1, tm, D), lambda b, i: (b, i, 0)),
            pl.BlockSpec((1, tm, D), lambda b, i: (b, i, 0)),
        ],
        out_shape=[
            jax.ShapeDtypeStruct((B, T, D), F32),
            jax.ShapeDtypeStruct((B, T, D), BF16),
        ],
        compiler_params=_params(("parallel", "parallel"), 52),
        name="outproj",
    )(yf, ym, x, row_tab, col_tab, mod, g, w)


def _mlp_kernel(hm_ref, w1_ref, w2_ref, x2_ref, mod_ref, g_ref, o_ref, acc):
    f = pl.program_id(2)

    @pl.when(f == 0)
    def _():
        acc[...] = jnp.zeros_like(acc)

    hid = jnp.square(jnp.maximum(_dot(hm_ref[0], w1_ref[...]), 0.0)).astype(BF16)
    acc[...] += _dot(hid, w2_ref[...])

    @pl.when(f == pl.num_programs(2) - 1)
    def _():
        o_ref[0] = _rms(x2_ref[0] + mod_ref[0, 5:6, :] * acc[...], g_ref[...])


def _mlp(hm, w1, w2, x2, mod, g, *, tm, tf):
    B, T, D = x2.shape
    dff = w1.shape[1]
    return pl.pallas_call(
        _mlp_kernel,
        grid=(B, T // tm, dff // tf),
        in_specs=[
            pl.BlockSpec((1, tm, D), lambda b, i, f: (b, i, 0)),
            pl.BlockSpec((D, tf), lambda b, i, f: (0, f)),
            pl.BlockSpec((tf, D), lambda b, i, f: (f, 0)),
            pl.BlockSpec((1, tm, D), lambda b, i, f: (b, i, 0)),
            pl.BlockSpec((1, N_MOD, D), lambda b, i, f: (b, 0, 0)),
            pl.BlockSpec((1, D), lambda b, i, f: (0, 0)),
        ],
        out_specs=pl.BlockSpec((1, tm, D), lambda b, i, f: (b, i, 0)),
        out_shape=jax.ShapeDtypeStruct((B, T, D), F32),
        scratch_shapes=[pltpu.VMEM((tm, D), F32)],
        compiler_params=_params(("parallel", "parallel", "arbitrary"), 56),
        name="mlp",
    )(hm, w1, w2, x2, mod, g)


def kernel(x, c, ctx, c_ctx, w_mod, b_mod, g_mix, g_mlp, w_in, conv_w, gate_b, head_g, w_out,
           w_mlp1, w_mlp2, g_final):
    assert w_mod.shape[0] == 1, "single-layer block"
    B, T, D = x.shape
    H = N_HEADS
    n_gate = N_DIRS * 2 * H
    mw = conv_w.shape[2] // 2
    fw = w_in.shape[2] - 4 * mw - n_gate
    gd = fw // N_GROUPS
    dh = mw // H
    assert T % GRID_W == 0 and fw % 1024 == 0 and mw == 1024 and dh % LANE == 0

    mod = _modulation(c, c_ctx, w_mod[0], b_mod[0])
    row_tab, col_tab = _pos_tables(T // GRID_W, D)
    g_mix2, g_mlp2, g_fin2 = (g.reshape(1, D) for g in (g_mix[0], g_mlp[0], g_final))

    perm = np.array([d * 2 * H + gate * H + h for gate in range(2) for d in range(N_DIRS) for h in range(H)])
    w_all = w_in[0].astype(BF16)
    w_gate = jnp.pad(w_in[0, :, fw + 4 * mw:][:, perm], ((0, 0), (0, LANE - n_gate))).astype(BF16)
    bias = jnp.pad(gate_b[0].reshape(n_gate)[perm], (0, LANE - n_gate)).reshape(1, LANE)

    p, gx = _inproj(x, row_tab, col_tab, mod, None, g_mix2, w_all, w_gate, use_pos=True, tm=512,
                    n_cols=fw + 4 * mw, col_blk=0, sigmoid_seg=(fw + 3 * mw) // 1024)
    assert (fw + mw) % (2 * mw) == 0
    pc, gc = _inproj(ctx, row_tab, col_tab, mod, B, g_mix2, w_all, w_gate, use_pos=False, tm=ctx.shape[1],
                     n_cols=2 * mw, col_blk=(fw + mw) // (2 * mw), sigmoid_seg=None)

    post = jnp.concatenate([jnp.full((1, mw), dh ** -0.5, F32), jnp.ones((1, mw), F32)], axis=1)
    conv = functools.partial(_conv_silu, conv_w=conv_w[0], post_scale=post)
    q = conv(p, col_blk=fw // 1024, wcol_blk=0, tc=512, transposed=False, width=dh)
    kt = conv(p, col_blk=(fw + mw) // 1024, wcol_blk=mw // 1024, tc=512, transposed=True, width=MLSTM_CHUNK)
    kct = conv(pc, col_blk=0, wcol_blk=mw // 1024, tc=ctx.shape[1], transposed=True, width=ctx.shape[1])

    rows, cols, crows = _gate_prep(gx, gc, bias, chunk=MLSTM_CHUNK)
    ym = _mlstm(q, kt, p, kct, pc, rows, cols, crows, head_g[0].reshape(1, mw),
                v_blk=(fw + 2 * mw) // dh, o_blk=(fw + 3 * mw) // dh, vc_blk=H)

    yf = _fourier_mix(p, _dft_tables(T, gd), gd=gd, n_groups=N_GROUPS)

    x2, hm = _outproj(yf, ym, x, row_tab, col_tab, mod, g_mlp2, w_out[0].astype(BF16), tm=512)
    return _mlp(hm, w_mlp1[0].astype(BF16), w_mlp2[0].astype(BF16), x2, mod, g_fin2, tm=512, tf=1024)
```
